```python
import math
import jax
import jax.numpy as jnp
from jax import lax
import numpy as np

D_MODEL = 2048
BATCH = 4
SEQ = 4096
DEPTH = 4

N_MIXERS = 2
N_HEADS = 16
HEAD_DIM = 128
Q_LORA = 512
KV_LORA = 256
IDX_HEADS = 16
IDX_DIM = 64
IDX_TOPK_MAX = 256
A_IN_COLS = Q_LORA + KV_LORA + IDX_DIM + IDX_HEADS
Q_BLOCK = 128
DILATED_PATTERNS = ((128, 1), (512, 4), (2048, 16))
BAND_BLOCK = 128
N_BUCKETS = 32
MAX_DISTANCE = 2048
N_EXPERTS = 16
N_GROUPS = 4
EXPERTS_PER_GROUP = N_EXPERTS // N_GROUPS
TOP_K = 2
D_EXPERT = 512
DEEPNORM_ALPHA = (2 * DEPTH) ** 0.25
DEEPNORM_BETA = (8 * DEPTH) ** -0.25
EPS = 1e-5

kernel_name = "hybrid_dsa_dilated_moe_deepnorm"


def t5_bucket(dist):
    n = jnp.maximum(dist, 0)
    max_exact = N_BUCKETS // 2
    nf = jnp.maximum(n, max_exact).astype(jnp.float32)
    large = max_exact + (jnp.log(nf / max_exact) / math.log(MAX_DISTANCE / max_exact)
                         * (N_BUCKETS - max_exact)).astype(jnp.int32)
    large = jnp.minimum(large, N_BUCKETS - 1)
    return jnp.where(n < max_exact, n, large)


def layer_norm(x, g, b):
    xf = x.astype(jnp.float32)
    mu = jnp.mean(xf, axis=-1, keepdims=True)
    var = jnp.mean(jnp.square(xf - mu), axis=-1, keepdims=True)
    y = (xf - mu) * lax.rsqrt(var + EPS) * g.astype(jnp.float32) + b.astype(jnp.float32)
    return y.astype(x.dtype)


def rms_norm(x, g):
    xf = x.astype(jnp.float32)
    y = xf * lax.rsqrt(jnp.mean(jnp.square(xf), axis=-1, keepdims=True) + EPS) * g.astype(jnp.float32)
    return y.astype(x.dtype)


def dsa_attention(x, w_in, q_norm, kv_norm, w_uq, w_uq_idx, w_uk, w_uv, w_o, rel_bias):
    bsz, seq, _ = x.shape
    proj = x @ w_in
    c_q, c_kv, k_idx, w_idx = jnp.split(
        proj, [Q_LORA, Q_LORA + KV_LORA, Q_LORA + KV_LORA + IDX_DIM], axis=-1)
    c_q = rms_norm(c_q, q_norm)
    c_kv = rms_norm(c_kv, kv_norm)
    q = (c_q @ w_uq).reshape(bsz, seq, N_HEADS, HEAD_DIM)
    q_lat = jnp.einsum('bthd,hdc->bthc', q, w_uk) * (HEAD_DIM ** -0.5)
    q_idx = (c_q @ w_uq_idx).reshape(bsz, seq, IDX_HEADS, IDX_DIM) * (IDX_DIM ** -0.5)
    w_idx = w_idx * (IDX_HEADS ** -0.5)
    top_k = min(IDX_TOPK_MAX, seq // 4)
    n_blk = seq // Q_BLOCK
    key_pos = jnp.arange(seq, dtype=jnp.int32)

    def blocks(a):
        return a.reshape((bsz, n_blk, Q_BLOCK) + a.shape[2:]).swapaxes(0, 1)

    def attend_block(args):
        ql, qi, wi, start = args
        t_pos = start + jnp.arange(Q_BLOCK, dtype=jnp.int32)
        rel = jax.nn.relu(jnp.einsum('bqhd,bsd->bqhs', qi, k_idx))
        score = jnp.einsum('bqh,bqhs->bqs', wi, rel).astype(jnp.float32)
        causal = key_pos[None, :] <= t_pos[:, None]
        score = jnp.where(causal[None], score, -jnp.inf)
        _, sel = lax.top_k(score, top_k)
        valid = sel <= t_pos[None, :, None]
        kv_sel = jax.vmap(lambda c, i: c[i])(c_kv, sel)
        logits = jnp.einsum('bqhc,bqkc->bqkh', ql, kv_sel).astype(jnp.float32)
        logits = logits + rel_bias[t5_bucket(t_pos[None, :, None] - sel)].astype(jnp.float32)
        logits = jnp.where(valid[..., None], logits, -jnp.inf)
        p = jax.nn.softmax(logits, axis=2).astype(kv_sel.dtype)
        return jnp.einsum('bqkh,bqkc->bqhc', p, kv_sel)

    starts = jnp.arange(n_blk, dtype=jnp.int32) * Q_BLOCK
    o_lat = lax.map(attend_block, (blocks(q_lat), blocks(q_idx), blocks(w_idx), starts))
    o_lat = o_lat.swapaxes(0, 1).reshape(bsz, seq, N_HEADS, KV_LORA)
    o = jnp.einsum('bthc,hcd->bthd', o_lat, w_uv).reshape(bsz, seq, N_HEADS * HEAD_DIM)
    return o @ w_o


def dilated_branch(q, k, v, window, dil, rel_bias):
    bsz, seq, heads, hd = q.shape
    ls = seq // dil
    look = window // dil
    nb = -(-ls // BAND_BLOCK)
    lp = nb * BAND_BLOCK

    def to_sub(a):
        return a.reshape(bsz, ls, dil, heads, hd).transpose(0, 2, 1, 3, 4)

    qs = jnp.pad(to_sub(q), ((0, 0), (0, 0), (0, lp - ls), (0, 0), (0, 0)))
    ks = jnp.pad(to_sub(k), ((0, 0), (0, 0), (look, lp - ls), (0, 0), (0, 0)))
    vs = jnp.pad(to_sub(v), ((0, 0), (0, 0), (look, lp - ls), (0, 0), (0, 0)))
    qb = qs.reshape(bsz, dil, nb, BAND_BLOCK, heads, hd)
    band = (jnp.arange(nb) * BAND_BLOCK)[:, None] + jnp.arange(BAND_BLOCK + look)[None, :]
    kb = ks[:, :, band]
    vb = vs[:, :, band]
    logits = jnp.einsum('bcnqhd,bcnkhd->bcnhqk', qb, kb).astype(jnp.float32)
    rel = jnp.arange(BAND_BLOCK)[:, None] + look - jnp.arange(BAND_BLOCK + look)[None, :]
    key_sub = band - look
    mask = ((rel >= 0) & (rel <= look))[None] & (key_sub >= 0)[:, None, :]
    bias = rel_bias[t5_bucket(dil * rel)].astype(jnp.float32).transpose(2, 0, 1)
    logits = jnp.where(mask[:, None], logits + bias, -jnp.inf)
    lse = jax.nn.logsumexp(logits, axis=-1)
    p = jnp.exp(logits - lse[..., None]).astype(v.dtype)
    o = jnp.einsum('bcnhqk,bcnkhd->bcnqhd', p, vb)
    o = o.reshape(bsz, dil, lp, heads, hd)[:, :, :ls].transpose(0, 2, 1, 3, 4).reshape(bsz, seq, heads, hd)
    lse = lse.transpose(0, 1, 2, 4, 3).reshape(bsz, dil, lp, heads)[:, :, :ls]
    lse = lse.transpose(0, 2, 1, 3).reshape(bsz, seq, heads)
    return o, lse


def dilated_attention(x, w_in, w_o, rel_bias):
    bsz, seq, _ = x.shape
    qkv = (x @ w_in).reshape(bsz, seq, 3, N_HEADS, HEAD_DIM)
    q = qkv[:, :, 0] * (HEAD_DIM ** -0.5)
    k = qkv[:, :, 1]
    v = qkv[:, :, 2]
    outs, lses = [], []
    for window, dil in DILATED_PATTERNS:
        o_i, lse_i = dilated_branch(q, k, v, window, dil, rel_bias)
        outs.append(o_i)
        lses.append(lse_i)
    wts = jax.nn.softmax(jnp.stack(lses, axis=0), axis=0)
    o = jnp.sum(wts[..., None].astype(v.dtype) * jnp.stack(outs, axis=0), axis=0)
    return o.reshape(bsz, seq, N_HEADS * HEAD_DIM) @ w_o


def grouped_moe(x, router_w, router_b, w_gate, w_up, w_down):
    bsz, seq, d = x.shape
    xt = x.reshape(-1, d)
    probs = jax.nn.softmax((xt @ router_w).astype(jnp.float32), axis=-1)
    sel = (probs + router_b.astype(jnp.float32)).reshape(-1, N_GROUPS, EXPERTS_PER_GROUP)
    top_in_group, _ = lax.top_k(sel, TOP_K)
    group = jnp.argmax(jnp.sum(top_in_group, axis=-1), axis=-1)
    in_group = jnp.take_along_axis(sel, group[:, None, None], axis=1)[:, 0]
    _, local = lax.top_k(in_group, TOP_K)
    expert = group[:, None] * EXPERTS_PER_GROUP + local
    gate = jnp.take_along_axis(probs, expert, axis=1)
    gate = gate / jnp.sum(gate, axis=-1, keepdims=True)
    combine = jnp.sum(jax.nn.one_hot(expert, N_EXPERTS, dtype=jnp.float32) * gate[..., None], axis=1)
    hidden = jax.nn.silu(jnp.einsum('nd,edf->nef', xt, w_gate)) * jnp.einsum('nd,edf->nef', xt, w_up)
    y = jnp.einsum('nef,efd->nd', hidden * combine[:, :, None].astype(xt.dtype), w_down)
    return y.reshape(bsz, seq, d)


def setup_inputs(seed: int = 0) -> dict:
    key = jax.random.key(seed)
    ks = jax.random.split(key, 24)
    n_a = (DEPTH + 1) // 2
    n_b = DEPTH // 2
    hw = N_HEADS * HEAD_DIM

    def nrm(k, shape, scale):
        return jax.random.normal(k, shape, jnp.float32) * scale

    return {
        "x": nrm(ks[0], (BATCH, SEQ, D_MODEL), 1.0),
        "rel_bias": nrm(ks[1], (N_BUCKETS, N_HEADS), 0.5),
        "router_w": nrm(ks[2], (D_MODEL, N_EXPERTS), D_MODEL ** -0.5),
        "router_b": nrm(ks[3], (N_EXPERTS,), 0.01),
        "a_w_in": nrm(ks[4], (n_a, D_MODEL, A_IN_COLS), D_MODEL ** -0.5),
        "a_q_norm": 1.0 + nrm(ks[5], (n_a, Q_LORA), 0.02),
        "a_kv_norm": 1.0 + nrm(ks[6], (n_a, KV_LORA), 0.02),
        "a_w_uq": nrm(ks[7], (n_a, Q_LORA, hw), Q_LORA ** -0.5),
        "a_w_uq_idx": nrm(ks[8], (n_a, Q_LORA, IDX_HEADS * IDX_DIM), Q_LORA ** -0.5),
        "a_w_uk": nrm(ks[9], (n_a, N_HEADS, HEAD_DIM, KV_LORA), HEAD_DIM ** -0.5),
        "a_w_uv": nrm(ks[10], (n_a, N_HEADS, KV_LORA, HEAD_DIM), KV_LORA ** -0.5),
        "a_w_o": nrm(ks[11], (n_a, hw, D_MODEL), hw ** -0.5 * DEEPNORM_BETA),
        "b_w_in": nrm(ks[12], (n_b, D_MODEL, 3 * hw), D_MODEL ** -0.5),
        "b_w_o": nrm(ks[13], (n_b, hw, D_MODEL), hw ** -0.5 * DEEPNORM_BETA),
        "ln1_g": 1.0 + nrm(ks[14], (DEPTH, D_MODEL), 0.02),
        "ln1_b": nrm(ks[15], (DEPTH, D_MODEL), 0.02),
        "ln2_g": 1.0 + nrm(ks[16], (DEPTH, D_MODEL), 0.02),
        "ln2_b": nrm(ks[17], (DEPTH, D_MODEL), 0.02),
        "moe_w_gate": nrm(ks[18], (DEPTH, N_EXPERTS, D_MODEL, D_EXPERT), D_MODEL ** -0.5),
        "moe_w_up": nrm(ks[19], (DEPTH, N_EXPERTS, D_MODEL, D_EXPERT), D_MODEL ** -0.5),
        "moe_w_down": nrm(ks[20], (DEPTH, N_EXPERTS, D_EXPERT, D_MODEL), D_EXPERT ** -0.5 * DEEPNORM_BETA),
    }


def reference(x, rel_bias, router_w, router_b, a_w_in, a_q_norm, a_kv_norm, a_w_uq, a_w_uq_idx,
              a_w_uk, a_w_uv, a_w_o, b_w_in, b_w_o, ln1_g, ln1_b, ln2_g, ln2_b,
              moe_w_gate, moe_w_up, moe_w_down):
    for i in range(DEPTH):
        j = i // N_MIXERS
        if i % N_MIXERS == 0:
            h = dsa_attention(x, a_w_in[j], a_q_norm[j], a_kv_norm[j], a_w_uq[j], a_w_uq_idx[j],
                              a_w_uk[j], a_w_uv[j], a_w_o[j], rel_bias)
        else:
            h = dilated_attention(x, b_w_in[j], b_w_o[j], rel_bias)
        x = layer_norm(DEEPNORM_ALPHA * x + h, ln1_g[i], ln1_b[i])
        y = grouped_moe(x, router_w, router_b, moe_w_gate[i], moe_w_up[i], moe_w_down[i])
        x = layer_norm(DEEPNORM_ALPHA * x + y, ln2_g[i], ln2_b[i])
    return x
```

```python
import functools
import math

import numpy as np
import jax
import jax.numpy as jnp
from jax import lax
from jax.experimental import pallas as pl
from jax.experimental.pallas import tpu as pltpu

F32 = jnp.float32
BF16 = jnp.bfloat16
I32 = jnp.int32

N_HEADS = 16
HEAD_DIM = 128
Q_LORA = 512
KV_LORA = 256
IDX_HEADS = 16
IDX_DIM = 64
IDX_TOPK_MAX = 256
DILATED_PATTERNS = ((128, 1), (512, 4), (2048, 16))
N_BUCKETS = 32
MAX_DISTANCE = 2048
N_EXPERTS = 16
N_GROUPS = 4
EXPERTS_PER_GROUP = N_EXPERTS // N_GROUPS
D_EXPERT = 512
EPS = 1e-5

LANES = 128
NEG = -1e30
INT_MIN = -(2 ** 31)
MASK_BUCKET = N_BUCKETS
VMEM_LIMIT = 56 * 1024 * 1024


def _cparams(*sem):
    return pltpu.CompilerParams(dimension_semantics=sem, vmem_limit_bytes=VMEM_LIMIT)


def _resident(shape, index_map):
    return pl.BlockSpec(shape, index_map, pipeline_mode=pl.Buffered(1))


def _t5_bucket(dist):
    n = jnp.maximum(dist, 0)
    max_exact = N_BUCKETS // 2
    nf = jnp.maximum(n, max_exact).astype(F32)
    large = max_exact + (jnp.log(nf / max_exact) / math.log(MAX_DISTANCE / max_exact)
                         * (N_BUCKETS - max_exact)).astype(I32)
    large = jnp.minimum(large, N_BUCKETS - 1)
    return jnp.where(n < max_exact, n, large)


def _layer_norm(v, g, b):
    mu = jnp.mean(v, axis=-1, keepdims=True)
    c = v - mu
    var = jnp.mean(c * c, axis=-1, keepdims=True)
    return c * lax.rsqrt(var + EPS) * g + b


def _rms_norm(v, g):
    return v * lax.rsqrt(jnp.mean(v * v, axis=-1, keepdims=True) + EPS) * g


def _bias_expand_body(bucket_ref, rb_ref, o_ref):
    bk = bucket_ref[0]
    for h in range(N_HEADS):
        acc = jnp.where(bk == MASK_BUCKET, NEG, 0.0).astype(F32)
        for b in range(N_BUCKETS):
            acc = jnp.where(bk == b, rb_ref[b, h], acc)
        o_ref[0, h] = acc


def bias_expand(bucket, rel_bias):
    n, r, c = bucket.shape
    return pl.pallas_call(
        _bias_expand_body,
        grid=(n,),
        in_specs=[pl.BlockSpec((1, r, c), lambda i: (i, 0, 0)),
                  pl.BlockSpec(memory_space=pltpu.SMEM)],
        out_specs=pl.BlockSpec((1, N_HEADS, r, c), lambda i: (i, 0, 0, 0)),
        out_shape=jax.ShapeDtypeStruct((n, N_HEADS, r, c), F32),
        compiler_params=_cparams("arbitrary"),
        name="bias_expand",
    )(bucket, rel_bias)


def _matmul_body(x_ref, w_ref, o_ref, *, scaled_blocks, scale):
    acc = jnp.dot(x_ref[...], w_ref[...], preferred_element_type=F32)
    if scaled_blocks:
        acc = acc * jnp.where(pl.program_id(0) < scaled_blocks, scale, 1.0).astype(F32)
    o_ref[...] = acc.astype(o_ref.dtype)


def matmul(x, w, out_dtype, tm, tn, scaled_blocks=0, scale=1.0):
    m, k = x.shape
    n = w.shape[1]
    return pl.pallas_call(
        functools.partial(_matmul_body, scaled_blocks=scaled_blocks, scale=scale),
        grid=(n // tn, m // tm),
        in_specs=[pl.BlockSpec((tm, k), lambda j, i: (i, 0)),
                  pl.BlockSpec((k, tn), lambda j, i: (0, j))],
        out_specs=pl.BlockSpec((tm, tn), lambda j, i: (i, j)),
        out_shape=jax.ShapeDtypeStruct((m, n), out_dtype),
        compiler_params=_cparams("arbitrary", "arbitrary"),
        name="matmul",
    )(x, w)


def _dsa_pro_body(x_ref, win_ref, qn_ref, kvn_ref, wuq_ref, wuqi_ref, wuk_ref,
                  qlat_ref, qidx_ref, widx_ref, ckv_ref, kidx_ref):
    proj = jnp.dot(x_ref[...], win_ref[...], preferred_element_type=F32)
    c_q = _rms_norm(proj[:, :Q_LORA], qn_ref[...]).astype(BF16)
    c_kv = _rms_norm(proj[:, Q_LORA:Q_LORA + KV_LORA], kvn_ref[...])
    ckv_ref[...] = c_kv.astype(BF16)
    kidx_ref[...] = proj[:, 768:768 + IDX_DIM].astype(BF16)
    widx_ref[...] = proj[:, 896:1024] * (IDX_HEADS ** -0.5)
    q = jnp.dot(c_q, wuq_ref[...], preferred_element_type=F32)
    for h in range(N_HEADS):
        qh = q[:, h * HEAD_DIM:(h + 1) * HEAD_DIM].astype(BF16)
        ql = jnp.dot(qh, wuk_ref[h], preferred_element_type=F32) * (HEAD_DIM ** -0.5)
        qlat_ref[h] = ql.astype(BF16)
    qi = jnp.dot(c_q, wuqi_ref[...], preferred_element_type=F32) * (IDX_DIM ** -0.5)
    for h in range(IDX_HEADS):
        qidx_ref[h] = qi[:, h * IDX_DIM:(h + 1) * IDX_DIM].astype(BF16)


def dsa_prologue(xb, w_in_p, q_norm, kv_norm, w_uq, w_uq_idx, w_uk, tm):
    n, d = xb.shape
    hw = N_HEADS * HEAD_DIM
    return pl.pallas_call(
        _dsa_pro_body,
        grid=(n // tm,),
        in_specs=[pl.BlockSpec((tm, d), lambda i: (i, 0)),
                  _resident((d, 1024), lambda i: (0, 0)),
                  _resident((1, Q_LORA), lambda i: (0, 0)),
                  _resident((1, KV_LORA), lambda i: (0, 0)),
                  _resident((Q_LORA, hw), lambda i: (0, 0)),
                  _resident((Q_LORA, IDX_HEADS * IDX_DIM), lambda i: (0, 0)),
                  _resident((N_HEADS, HEAD_DIM, KV_LORA), lambda i: (0, 0, 0))],
        out_specs=[pl.BlockSpec((N_HEADS, tm, KV_LORA), lambda i: (0, i, 0)),
                   pl.BlockSpec((IDX_HEADS, tm, IDX_DIM), lambda i: (0, i, 0)),
                   pl.BlockSpec((tm, LANES), lambda i: (i, 0)),
                   pl.BlockSpec((tm, KV_LORA), lambda i: (i, 0)),
                   pl.BlockSpec((tm, IDX_DIM), lambda i: (i, 0))],
        out_shape=[jax.ShapeDtypeStruct((N_HEADS, n, KV_LORA), BF16),
                   jax.ShapeDtypeStruct((IDX_HEADS, n, IDX_DIM), BF16),
                   jax.ShapeDtypeStruct((n, LANES), F32),
                   jax.ShapeDtypeStruct((n, KV_LORA), BF16),
                   jax.ShapeDtypeStruct((n, IDX_DIM), BF16)],
        compiler_params=_cparams("arbitrary"),
        name="dsa_prologue",
    )(xb, w_in_p, q_norm, kv_norm, w_uq, w_uq_idx, w_uk)


DSA_TQ = 128
DSA_TK = 256


def _dsa_attn_body(qlat_ref, qidx_ref, widx_ref, ckv_ref, kidx_ref, tab_ref, wuv_ref, o_ref,
                   keys_ref, acc_ref, m_ref, l_ref, wib_ref, pstar_ref, *, seq, topk, n_tab):
    tq, tk = DSA_TQ, DSA_TK
    i = pl.program_id(1)
    q0 = i * tq
    nk = (q0 + tq + tk - 1) // tk
    qpos = q0 + lax.broadcasted_iota(I32, (tq, tk), 0)
    col = lax.broadcasted_iota(I32, (tq, tk), 1)

    def dup(v):
        return jnp.concatenate([v] * (tk // LANES), axis=1)

    for h in range(IDX_HEADS):
        wib_ref[h] = jnp.broadcast_to(widx_ref[:, h:h + 1], (tq, LANES))

    def score_tile(j, carry):
        k0 = pl.multiple_of(j * tk, tk)
        kt = kidx_ref[pl.ds(k0, tk), :]
        s = jnp.zeros((tq, tk), F32)
        for h in range(IDX_HEADS):
            rel = lax.dot_general(qidx_ref[h], kt, (((1,), (1,)), ((), ())),
                                  preferred_element_type=F32)
            s = s + jnp.maximum(rel, 0.0) * dup(wib_ref[h])
        bits = pltpu.bitcast(s, I32)
        key = bits ^ ((bits >> 31) & 0x7FFFFFFF)
        keys_ref[j] = jnp.where(k0 + col <= qpos, key, INT_MIN)
        return carry

    lax.fori_loop(0, nk, score_tile, 0)

    def count(pred):
        def body(j, acc):
            ind = jnp.where(pred(keys_ref[j], j), 1.0, 0.0)
            part = ind[:, :LANES]
            for c in range(1, tk // LANES):
                part = part + ind[:, c * LANES:(c + 1) * LANES]
            return acc + part
        acc = lax.fori_loop(0, nk, body, jnp.zeros((tq, LANES), F32))
        return jnp.sum(acc, axis=1, keepdims=True)

    def bit_step(t, carry):
        v, cge = carry
        cand_u = v | jnp.left_shift(jnp.int32(1), 31 - t)
        cand = dup(cand_u ^ INT_MIN)
        cnt = count(lambda key, j: key >= cand)
        take = cnt >= topk
        return jnp.where(take, cand_u, v), jnp.where(take, cnt, cge)

    v, cge = lax.fori_loop(0, 32, bit_step,
                           (jnp.zeros((tq, LANES), I32), jnp.zeros((tq, 1), F32)))
    thr = v ^ INT_MIN
    thr_w = dup(thr)

    pstar_ref[...] = jnp.full((tq, LANES), seq, I32)
    surplus = jnp.where((v[:, :1] != 0) & (cge > topk), 1.0, 0.0)

    @pl.when(jnp.max(surplus) > 0.0)
    def _():
        cgt = count(lambda key, j: key > thr_w)
        need = topk - cgt

        def pos_step(t, p):
            cand = p | jnp.left_shift(jnp.int32(1), (seq.bit_length() - 1) - t)
            cw = dup(cand)
            before = count(lambda key, j: (key == thr_w) & (j * tk + col < cw))
            return jnp.where(before < need, cand, p)

        p = lax.fori_loop(0, seq.bit_length(), pos_step, jnp.zeros((tq, LANES), I32))
        pstar_ref[...] = jnp.where(surplus > 0.0, p, seq)

    pstar_w = dup(pstar_ref[...])

    acc_ref[...] = jnp.zeros_like(acc_ref)
    m_ref[...] = jnp.full_like(m_ref, NEG)
    l_ref[...] = jnp.zeros_like(l_ref)
    dq = q0 // LANES

    def attn_tile(j, carry):
        k0 = pl.multiple_of(j * tk, tk)
        key = keys_ref[j]
        kpos = k0 + col
        sel = (kpos <= qpos) & ((key > thr_w) | ((key == thr_w) & (kpos <= pstar_w)))
        maskadd = jnp.where(sel, 0.0, NEG)
        kv = ckv_ref[pl.ds(k0, tk), :]
        dbase = dq - j * (tk // LANES) + 1
        didx = [jnp.clip(dbase - c, 0, n_tab - 1) for c in range(tk // LANES)]

        def head(h, hc):
            s = lax.dot_general(qlat_ref[h], kv, (((1,), (1,)), ((), ())),
                                preferred_element_type=F32)
            bias = jnp.concatenate([tab_ref[d, h] for d in didx], axis=1)
            s = s + bias + maskadd
            m_old = m_ref[h]
            m_new = jnp.maximum(m_old, jnp.max(s, axis=1, keepdims=True))
            alpha = jnp.exp(m_old - m_new)
            p = jnp.exp(s - m_new)
            l_ref[h] = alpha * l_ref[h] + jnp.sum(p, axis=1, keepdims=True)
            acc_ref[h] = alpha * acc_ref[h] + jnp.dot(p.astype(BF16), kv,
                                                      preferred_element_type=F32)
            m_ref[h] = m_new
            return hc

        lax.fori_loop(0, N_HEADS, head, 0)
        return carry

    lax.fori_loop(0, nk, attn_tile, 0)

    outs = []
    for h in range(N_HEADS):
        o_lat = acc_ref[h] * (1.0 / l_ref[h])
        outs.append(jnp.dot(o_lat.astype(BF16), wuv_ref[h], preferred_element_type=F32))
    o_ref[...] = jnp.concatenate(outs, axis=1).astype(o_ref.dtype)


def dsa_attention(qlat, qidx, widx, ckv, kidx, tab, w_uv, bsz, seq):
    tq, tk = DSA_TQ, DSA_TK
    nq = seq // tq
    n = bsz * seq
    n_tab = tab.shape[0]
    topk = min(IDX_TOPK_MAX, seq // 4)
    body = functools.partial(_dsa_attn_body, seq=seq, topk=topk, n_tab=n_tab)
    return pl.pallas_call(
        body,
        grid=(bsz, nq),
        in_specs=[pl.BlockSpec((N_HEADS, tq, KV_LORA), lambda b, i: (0, b * nq + i, 0)),
                  pl.BlockSpec((IDX_HEADS, tq, IDX_DIM), lambda b, i: (0, b * nq + i, 0)),
                  pl.BlockSpec((tq, LANES), lambda b, i: (b * nq + i, 0)),
                  pl.BlockSpec((seq, KV_LORA), lambda b, i: (b, 0)),
                  pl.BlockSpec((seq, IDX_DIM), lambda b, i: (b, 0)),
                  _resident(tab.shape, lambda b, i: (0, 0, 0, 0)),
                  _resident((N_HEADS, KV_LORA, HEAD_DIM), lambda b, i: (0, 0, 0))],
        out_specs=pl.BlockSpec((tq, N_HEADS * HEAD_DIM), lambda b, i: (b * nq + i, 0)),
        out_shape=jax.ShapeDtypeStruct((n, N_HEADS * HEAD_DIM), BF16),
        scratch_shapes=[pltpu.VMEM((seq // tk, tq, tk), I32),
                        pltpu.VMEM((N_HEADS, tq, KV_LORA), F32),
                        pltpu.VMEM((N_HEADS, tq, 1), F32),
                        pltpu.VMEM((N_HEADS, tq, 1), F32),
                        pltpu.VMEM((IDX_HEADS, tq, LANES), F32),
                        pltpu.VMEM((tq, LANES), I32)],
        compiler_params=_cparams("arbitrary", "arbitrary"),
        name="dsa_attention",
    )(qlat, qidx, widx, ckv, kidx, tab, w_uv)


def dsa_bias_buckets(seq):
    far = np.maximum(np.arange(seq), N_BUCKETS // 2).astype(np.float64)
    far_bucket = np.floor(np.log(far / (N_BUCKETS // 2)) / math.log(MAX_DISTANCE / (N_BUCKETS // 2))
                          * (N_BUCKETS - N_BUCKETS // 2))
    unsat = np.nonzero(far_bucket < N_BUCKETS - 1 - N_BUCKETS // 2)[0]
    n_sat = int(unsat[-1]) + 1 if unsat.size else 0
    n_tab = min(seq // LANES + 1, -(-(n_sat + LANES - 1) // LANES) + 2)
    t = np.arange(n_tab)[:, None, None]
    qi = np.arange(LANES)[None, :, None]
    ki = np.arange(LANES)[None, None, :]
    dist = np.maximum(LANES * (t - 1) + qi - ki, 0)
    return _t5_bucket(jnp.asarray(dist, I32))


def _dil_body(q_ref, kp_ref, km_ref, vp_ref, vm_ref, bias_ref, o_ref, lse_ref, *, look):
    nblk = pl.program_id(2)
    tq = q_ref.shape[1]
    ncol = look + tq
    colv = lax.broadcasted_iota(I32, (1, ncol), 1)
    first = jnp.where((nblk == 0) & (colv < look), NEG, 0.0)
    lse_ref[0] = jnp.zeros(lse_ref.shape[1:], F32)
    outs = []
    for h in range(N_HEADS):
        sl = slice(h * HEAD_DIM, (h + 1) * HEAD_DIM)
        q = q_ref[0, :, sl]
        k = jnp.concatenate([kp_ref[0, :, sl], km_ref[0, :, sl]], axis=0)
        v = jnp.concatenate([vp_ref[0, :, sl], vm_ref[0, :, sl]], axis=0)
        s = lax.dot_general(q, k, (((1,), (1,)), ((), ())), preferred_element_type=F32)
        s = s + bias_ref[0, h] + first
        m = jnp.max(s, axis=1, keepdims=True)
        p = jnp.exp(s - m)
        l = jnp.sum(p, axis=1, keepdims=True)
        o = jnp.dot(p.astype(BF16), v, preferred_element_type=F32) * (1.0 / l)
        outs.append(o)
        lse_ref[0, :, h:h + 1] = m + jnp.log(l)
    o_ref[0] = jnp.concatenate(outs, axis=1).astype(o_ref.dtype)


def dilated_branch(qkv, bias, bsz, seq, dil, look):
    hw = N_HEADS * HEAD_DIM
    ls = seq // dil
    tq = bias.shape[2]
    nb = ls // tq
    r = tq // look
    view = qkv.reshape(bsz, ls, dil * 3 * hw)
    prev = lambda part: (lambda b, c, n: (b, jnp.maximum(n * r - 1, 0), c * 3 + part))
    main = lambda part: (lambda b, c, n: (b, n, c * 3 + part))
    o, lse = pl.pallas_call(
        functools.partial(_dil_body, look=look),
        grid=(bsz, dil, nb),
        in_specs=[pl.BlockSpec((1, tq, hw), main(0)),
                  pl.BlockSpec((1, look, hw), prev(1)),
                  pl.BlockSpec((1, tq, hw), main(1)),
                  pl.BlockSpec((1, look, hw), prev(2)),
                  pl.BlockSpec((1, tq, hw), main(2)),
                  _resident(bias.shape, lambda b, c, n: (0, 0, 0, 0))],
        out_specs=[pl.BlockSpec((1, tq, hw), lambda b, c, n: (b, n, c)),
                   pl.BlockSpec((1, tq, LANES), lambda b, c, n: (b, n, c))],
        out_shape=[jax.ShapeDtypeStruct((bsz, ls, dil * hw), BF16),
                   jax.ShapeDtypeStruct((bsz, ls, dil * LANES), F32)],
        compiler_params=_cparams("arbitrary", "arbitrary", "arbitrary"),
        name=f"dilated_d{dil}",
    )(view, view, view, view, view, bias)
    return o.reshape(bsz * seq, hw), lse.reshape(bsz * seq, LANES)


def dilated_bias_buckets(tq, look, dil):
    rel = np.arange(tq)[:, None] + look - np.arange(look + tq)[None, :]
    band = (rel >= 0) & (rel <= look)
    bucket = _t5_bucket(jnp.asarray(dil * np.maximum(rel, 0), I32))
    return jnp.where(jnp.asarray(band), bucket, MASK_BUCKET)[None]


def _post_attn_body(*refs, n_parts, alpha):
    o_refs = refs[:n_parts]
    lse_refs = refs[n_parts:2 * n_parts] if n_parts > 1 else ()
    base = n_parts + len(lse_refs)
    wo_ref, x_ref, g_ref, b_ref, xo_ref, xb_ref = refs[base:base + 6]
    if n_parts == 1:
        o = o_refs[0][...]
    else:
        lses = [r[...] for r in lse_refs]
        mx = functools.reduce(jnp.maximum, lses)
        ws = [jnp.exp(v - mx) for v in lses]
        inv = 1.0 / functools.reduce(lambda a, b: a + b, ws)
        ws = [w * inv for w in ws]
        cols = []
        for h in range(N_HEADS):
            sl = slice(h * HEAD_DIM, (h + 1) * HEAD_DIM)
            acc = None
            for part in range(n_parts):
                term = ws[part][:, h:h + 1] * o_refs[part][:, sl].astype(F32)
                acc = term if acc is None else acc + term
            cols.append(acc)
        o = jnp.concatenate(cols, axis=1).astype(BF16)
    hproj = jnp.dot(o, wo_ref[...], preferred_element_type=F32)
    y = _layer_norm(alpha * x_ref[...] + hproj, g_ref[...], b_ref[...])
    xo_ref[...] = y
    xb_ref[...] = y.astype(BF16)


def post_attention(o_parts, lse_parts, w_o, x, g, b, alpha, tm):
    n, d = x.shape
    hw = N_HEADS * HEAD_DIM
    n_parts = len(o_parts)
    row = lambda width: pl.BlockSpec((tm, width), lambda i: (i, 0))
    in_specs = ([row(hw)] * n_parts + [row(LANES)] * len(lse_parts)
                + [_resident((hw, d), lambda i: (0, 0)), row(d),
                   _resident((1, d), lambda i: (0, 0)), _resident((1, d), lambda i: (0, 0))])
    return pl.pallas_call(
        functools.partial(_post_attn_body, n_parts=n_parts, alpha=alpha),
        grid=(n // tm,),
        in_specs=in_specs,
        out_specs=[row(d), row(d)],
        out_shape=[jax.ShapeDtypeStruct((n, d), F32), jax.ShapeDtypeStruct((n, d), BF16)],
        compiler_params=_cparams("arbitrary"),
        name=f"post_attention_{n_parts}",
    )(*o_parts, *lse_parts, w_o, x, g, b)


def _router_body(x_ref, rwt_ref, rb_ref, comb_ref):
    tm = x_ref.shape[0]
    logits = lax.dot_general(rwt_ref[...], x_ref[...], (((1,), (1,)), ((), ())),
                             preferred_element_type=F32,
                             precision=lax.Precision.HIGHEST)
    rows = [logits[e:e + 1, :] for e in range(N_EXPERTS)]
    mx = functools.reduce(jnp.maximum, rows)
    ex = [jnp.exp(r - mx) for r in rows]
    inv = 1.0 / functools.reduce(lambda a, b: a + b, ex)
    probs = [e * inv for e in ex]
    sel = [probs[e] + rb_ref[e] for e in range(N_EXPERTS)]

    zero_i = jnp.zeros((1, tm), I32)
    best_t = None
    for g in range(N_GROUPS):
        vals = sel[g * EXPERTS_PER_GROUP:(g + 1) * EXPERTS_PER_GROUP]
        prb = probs[g * EXPERTS_PER_GROUP:(g + 1) * EXPERTS_PER_GROUP]
        top, top_i, top_p = vals[0], zero_i, prb[0]
        for j in range(1, EXPERTS_PER_GROUP):
            c = vals[j] > top
            top = jnp.where(c, vals[j], top)
            top_i = jnp.where(c, j, top_i)
            top_p = jnp.where(c, prb[j], top_p)
        sec = jnp.full((1, tm), -jnp.inf, F32)
        sec_i, sec_p = zero_i, jnp.zeros((1, tm), F32)
        for j in range(EXPERTS_PER_GROUP):
            c = (vals[j] > sec) & (top_i != j)
            sec = jnp.where(c, vals[j], sec)
            sec_i = jnp.where(c, j, sec_i)
            sec_p = jnp.where(c, prb[j], sec_p)
        tot = top + sec
        if best_t is None:
            best_t = tot
            e0, e1 = top_i + g * EXPERTS_PER_GROUP, sec_i + g * EXPERTS_PER_GROUP
            p0, p1 = top_p, sec_p
        else:
            c = tot > best_t
            best_t = jnp.where(c, tot, best_t)
            e0 = jnp.where(c, top_i + g * EXPERTS_PER_GROUP, e0)
            e1 = jnp.where(c, sec_i + g * EXPERTS_PER_GROUP, e1)
            p0 = jnp.where(c, top_p, p0)
            p1 = jnp.where(c, sec_p, p1)
    ginv = 1.0 / (p0 + p1)
    g0, g1 = p0 * ginv, p1 * ginv
    comb_rows = [jnp.where(e0 == e, g0, 0.0) + jnp.where(e1 == e, g1, 0.0)
                 for e in range(N_EXPERTS)]
    comb_rows.append(jnp.zeros((LANES - N_EXPERTS, tm), F32))
    comb_ref[...] = jnp.concatenate(comb_rows, axis=0).T


def router(x, router_wt, router_b, tm):
    n, d = x.shape
    return pl.pallas_call(
        _router_body,
        grid=(n // tm,),
        in_specs=[pl.BlockSpec((tm, d), lambda i: (i, 0)),
                  _resident((N_EXPERTS, d), lambda i: (0, 0)),
                  pl.BlockSpec(memory_space=pltpu.SMEM)],
        out_specs=pl.BlockSpec((tm, LANES), lambda i: (i, 0)),
        out_shape=jax.ShapeDtypeStruct((n, LANES), F32),
        compiler_params=_cparams("arbitrary"),
        name="router",
    )(x, router_wt, router_b)


def _moe_body(xb_ref, comb_ref, wg_ref, wu_ref, wd_ref, x_ref, g_ref, b_ref,
              xo_ref, xbo_ref, acc_ref, *, alpha):
    e = pl.program_id(1)

    @pl.when(e == 0)
    def _():
        acc_ref[...] = jnp.zeros_like(acc_ref)

    xb = xb_ref[...]
    gate = jnp.dot(xb, wg_ref[0], preferred_element_type=F32)
    up = jnp.dot(xb, wu_ref[0], preferred_element_type=F32)
    lane = lax.broadcasted_iota(I32, comb_ref.shape, 1)
    c = jnp.sum(jnp.where(lane == e, comb_ref[...], 0.0), axis=1, keepdims=True)
    hidden = (gate * jax.nn.sigmoid(gate)) * up * c
    acc_ref[...] += jnp.dot(hidden.astype(BF16), wd_ref[0], preferred_element_type=F32)

    @pl.when(e == N_EXPERTS - 1)
    def _():
        y = _layer_norm(alpha * x_ref[...] + acc_ref[...], g_ref[...], b_ref[...])
        xo_ref[...] = y
        xbo_ref[...] = y.astype(BF16)


def moe_dense(xb, comb, w_gate, w_up, w_down, x, g, b, alpha, tm):
    n, d = x.shape
    row = lambda width: pl.BlockSpec((tm, width), lambda i, e: (i, 0))
    return pl.pallas_call(
        functools.partial(_moe_body, alpha=alpha),
        grid=(n // tm, N_EXPERTS),
        in_specs=[row(d), row(LANES),
                  pl.BlockSpec((1, d, D_EXPERT), lambda i, e: (e, 0, 0)),
                  pl.BlockSpec((1, d, D_EXPERT), lambda i, e: (e, 0, 0)),
                  pl.BlockSpec((1, D_EXPERT, d), lambda i, e: (e, 0, 0)),
                  row(d),
                  _resident((1, d), lambda i, e: (0, 0)), _resident((1, d), lambda i, e: (0, 0))],
        out_specs=[row(d), row(d)],
        out_shape=[jax.ShapeDtypeStruct((n, d), F32), jax.ShapeDtypeStruct((n, d), BF16)],
        scratch_shapes=[pltpu.VMEM((tm, d), F32)],
        compiler_params=_cparams("arbitrary", "arbitrary"),
        name="moe_dense",
    )(xb, comb, w_gate, w_up, w_down, x, g, b)


def kernel(x, rel_bias, router_w, router_b, a_w_in, a_q_norm, a_kv_norm, a_w_uq, a_w_uq_idx,
           a_w_uk, a_w_uv, a_w_o, b_w_in, b_w_o, ln1_g, ln1_b, ln2_g, ln2_b,
           moe_w_gate, moe_w_up, moe_w_down):
    bsz, seq, d = x.shape
    depth = ln1_g.shape[0]
    n = bsz * seq
    hw = N_HEADS * HEAD_DIM
    alpha = (2 * depth) ** 0.25

    x = x.reshape(n, d)
    xb = x.astype(BF16)
    router_wt = router_w.T

    dsa_tab = bias_expand(dsa_bias_buckets(seq), rel_bias)
    dil_bias = []
    for window, dil in DILATED_PATTERNS:
        look = window // dil
        tq = min(2 * look, seq // dil)
        dil_bias.append(bias_expand(dilated_bias_buckets(tq, look, dil), rel_bias))

    for i in range(depth):
        j = i // 2
        if i % 2 == 0:
            w_in = a_w_in[j]
            kv_end = Q_LORA + KV_LORA
            w_in_p = jnp.concatenate(
                [w_in[:, :kv_end + IDX_DIM], jnp.zeros((d, LANES - IDX_DIM), F32),
                 w_in[:, kv_end + IDX_DIM:], jnp.zeros((d, LANES - IDX_HEADS), F32)],
                axis=1).astype(BF16)
            qlat, qidx, widx, ckv, kidx = dsa_prologue(
                xb, w_in_p, a_q_norm[j][None], a_kv_norm[j][None], a_w_uq[j].astype(BF16),
                a_w_uq_idx[j].astype(BF16), a_w_uk[j].astype(BF16), tm=256)
            o = dsa_attention(qlat, qidx, widx, ckv, kidx, dsa_tab, a_w_uv[j].astype(BF16),
                              bsz, seq)
            o_parts, lse_parts, w_o = [o], [], a_w_o[j]
        else:
            qkv = matmul(xb, b_w_in[j].astype(BF16), BF16, tm=512, tn=1024,
                         scaled_blocks=hw // 1024, scale=HEAD_DIM ** -0.5)
            o_parts, lse_parts = [], []
            for (window, dil), bias in zip(DILATED_PATTERNS, dil_bias):
                o_i, lse_i = dilated_branch(qkv, bias, bsz, seq, dil, window // dil)
                o_parts.append(o_i)
                lse_parts.append(lse_i)
            w_o = b_w_o[j]
        x, xb = post_attention(o_parts, lse_parts, w_o.astype(BF16), x, ln1_g[i][None],
                               ln1_b[i][None], alpha, tm=256)
        comb = router(x, router_wt, router_b, tm=512)
        x, xb = moe_dense(xb, comb, moe_w_gate[i].astype(BF16), moe_w_up[i].astype(BF16),
                          moe_w_down[i].astype(BF16), x, ln2_g[i][None], ln2_b[i][None],
                          alpha, tm=512)
    return x.reshape(bsz, seq, d)
```

```python
import functools
import math

import numpy as np
import jax
import jax.numpy as jnp
from jax import lax
from jax.experimental import pallas as pl
from jax.experimental.pallas import tpu as pltpu

F32 = jnp.float32
BF16 = jnp.bfloat16
I32 = jnp.int32

N_HEADS = 16
HEAD_DIM = 128
Q_LORA = 512
KV_LORA = 256
IDX_HEADS = 16
IDX_DIM = 64
IDX_TOPK_MAX = 256
DILATED_PATTERNS = ((128, 1), (512, 4), (2048, 16))
N_BUCKETS = 32
MAX_DISTANCE = 2048
N_EXPERTS = 16
N_GROUPS = 4
EXPERTS_PER_GROUP = N_EXPERTS // N_GROUPS
D_EXPERT = 512
EPS = 1e-5

LANES = 128
NEG = -1e30
INT_MIN = -(2 ** 31)
MASK_BUCKET = N_BUCKETS
VMEM_LIMIT = 56 * 1024 * 1024


def _cparams(*sem):
    return pltpu.CompilerParams(dimension_semantics=sem, vmem_limit_bytes=VMEM_LIMIT)


def _resident(shape, index_map):
    return pl.BlockSpec(shape, index_map, pipeline_mode=pl.Buffered(1))


def _t5_bucket(dist):
    n = jnp.maximum(dist, 0)
    max_exact = N_BUCKETS // 2
    nf = jnp.maximum(n, max_exact).astype(F32)
    large = max_exact + (jnp.log(nf / max_exact) / math.log(MAX_DISTANCE / max_exact)
                         * (N_BUCKETS - max_exact)).astype(I32)
    large = jnp.minimum(large, N_BUCKETS - 1)
    return jnp.where(n < max_exact, n, large)


def _layer_norm(v, g, b):
    mu = jnp.mean(v, axis=-1, keepdims=True)
    c = v - mu
    var = jnp.mean(c * c, axis=-1, keepdims=True)
    return c * lax.rsqrt(var + EPS) * g + b


def _rms_norm(v, g):
    return v * lax.rsqrt(jnp.mean(v * v, axis=-1, keepdims=True) + EPS) * g


def _bias_expand_body(bucket_ref, rb_ref, o_ref):
    bk = bucket_ref[0]
    for h in range(N_HEADS):
        acc = jnp.where(bk == MASK_BUCKET, NEG, 0.0).astype(F32)
        for b in range(N_BUCKETS):
            acc = jnp.where(bk == b, rb_ref[b, h], acc)
        o_ref[0, h] = acc


def bias_expand(bucket, rel_bias):
    n, r, c = bucket.shape
    return pl.pallas_call(
        _bias_expand_body,
        grid=(n,),
        in_specs=[pl.BlockSpec((1, r, c), lambda i: (i, 0, 0)),
                  pl.BlockSpec(memory_space=pltpu.SMEM)],
        out_specs=pl.BlockSpec((1, N_HEADS, r, c), lambda i: (i, 0, 0, 0)),
        out_shape=jax.ShapeDtypeStruct((n, N_HEADS, r, c), F32),
        compiler_params=_cparams("arbitrary"),
        name="bias_expand",
    )(bucket, rel_bias)


def _matmul_body(x_ref, w_ref, o_ref, *, scaled_blocks, scale):
    acc = jnp.dot(x_ref[...], w_ref[...], preferred_element_type=F32)
    if scaled_blocks:
        acc = acc * jnp.where(pl.program_id(0) < scaled_blocks, scale, 1.0).astype(F32)
    o_ref[...] = acc.astype(o_ref.dtype)


def matmul(x, w, out_dtype, tm, tn, scaled_blocks=0, scale=1.0):
    m, k = x.shape
    n = w.shape[1]
    return pl.pallas_call(
        functools.partial(_matmul_body, scaled_blocks=scaled_blocks, scale=scale),
        grid=(n // tn, m // tm),
        in_specs=[pl.BlockSpec((tm, k), lambda j, i: (i, 0)),
                  pl.BlockSpec((k, tn), lambda j, i: (0, j))],
        out_specs=pl.BlockSpec((tm, tn), lambda j, i: (i, j)),
        out_shape=jax.ShapeDtypeStruct((m, n), out_dtype),
        compiler_params=_cparams("arbitrary", "arbitrary"),
        name="matmul",
    )(x, w)


def _dsa_pro_body(x_ref, win_ref, qn_ref, kvn_ref, wuq_ref, wuqi_ref, wuk_ref,
                  qlat_ref, qidx_ref, widx_ref, ckv_ref, kidx_ref):
    proj = jnp.dot(x_ref[...], win_ref[...], preferred_element_type=F32)
    c_q = _rms_norm(proj[:, :Q_LORA], qn_ref[...]).astype(BF16)
    c_kv = _rms_norm(proj[:, Q_LORA:Q_LORA + KV_LORA], kvn_ref[...])
    ckv_ref[...] = c_kv.astype(BF16)
    kidx_ref[...] = proj[:, 768:768 + IDX_DIM].astype(BF16)
    widx_ref[...] = proj[:, 896:1024] * (IDX_HEADS ** -0.5)
    q = jnp.dot(c_q, wuq_ref[...], preferred_element_type=F32)
    for h in range(N_HEADS):
        qh = q[:, h * HEAD_DIM:(h + 1) * HEAD_DIM].astype(BF16)
        ql = jnp.dot(qh, wuk_ref[h], preferred_element_type=F32) * (HEAD_DIM ** -0.5)
        qlat_ref[h] = ql.astype(BF16)
    qi = jnp.dot(c_q, wuqi_ref[...], preferred_element_type=F32) * (IDX_DIM ** -0.5)
    for h in range(IDX_HEADS):
        qidx_ref[h] = qi[:, h * IDX_DIM:(h + 1) * IDX_DIM].astype(BF16)


def dsa_prologue(xb, w_in_p, q_norm, kv_norm, w_uq, w_uq_idx, w_uk, tm):
    n, d = xb.shape
    hw = N_HEADS * HEAD_DIM
    return pl.pallas_call(
        _dsa_pro_body,
        grid=(n // tm,),
        in_specs=[pl.BlockSpec((tm, d), lambda i: (i, 0)),
                  _resident((d, 1024), lambda i: (0, 0)),
                  _resident((1, Q_LORA), lambda i: (0, 0)),
                  _resident((1, KV_LORA), lambda i: (0, 0)),
                  _resident((Q_LORA, hw), lambda i: (0, 0)),
                  _resident((Q_LORA, IDX_HEADS * IDX_DIM), lambda i: (0, 0)),
                  _resident((N_HEADS, HEAD_DIM, KV_LORA), lambda i: (0, 0, 0))],
        out_specs=[pl.BlockSpec((N_HEADS, tm, KV_LORA), lambda i: (0, i, 0)),
                   pl.BlockSpec((IDX_HEADS, tm, IDX_DIM), lambda i: (0, i, 0)),
                   pl.BlockSpec((tm, LANES), lambda i: (i, 0)),
                   pl.BlockSpec((tm, KV_LORA), lambda i: (i, 0)),
                   pl.BlockSpec((tm, IDX_DIM), lambda i: (i, 0))],
        out_shape=[jax.ShapeDtypeStruct((N_HEADS, n, KV_LORA), BF16),
                   jax.ShapeDtypeStruct((IDX_HEADS, n, IDX_DIM), BF16),
                   jax.ShapeDtypeStruct((n, LANES), F32),
                   jax.ShapeDtypeStruct((n, KV_LORA), BF16),
                   jax.ShapeDtypeStruct((n, IDX_DIM), BF16)],
        compiler_params=_cparams("arbitrary"),
        name="dsa_prologue",
    )(xb, w_in_p, q_norm, kv_norm, w_uq, w_uq_idx, w_uk)


DSA_TQ = 128
DSA_TK = 256


def _dsa_attn_body(qlat_ref, qidx_ref, widx_ref, ckv_ref, kidx_ref, tab_ref, wuv_ref, o_ref,
                   keys_ref, acc_ref, m_ref, wib_ref, pstar_ref,
                   s0_ref, s1_ref, p0_ref, p1_ref, a0_ref, a1_ref,
                   *, seq, topk, n_tab):
    tq, tk = DSA_TQ, DSA_TK
    i = pl.program_id(1)
    q0 = i * tq
    nk = (q0 + tq + tk - 1) // tk
    qpos = q0 + lax.broadcasted_iota(I32, (tq, tk), 0)
    col = lax.broadcasted_iota(I32, (tq, tk), 1)

    def dup(v):
        return jnp.concatenate([v] * (tk // LANES), axis=1)

    for h in range(IDX_HEADS):
        wib_ref[h] = jnp.broadcast_to(widx_ref[:, h:h + 1], (tq, LANES))

    def score_tile(j, carry):
        k0 = pl.multiple_of(j * tk, tk)
        kt = kidx_ref[pl.ds(k0, tk), :]
        s = jnp.zeros((tq, tk), F32)
        for h in range(IDX_HEADS):
            rel = lax.dot_general(qidx_ref[h], kt, (((1,), (1,)), ((), ())),
                                  preferred_element_type=F32)
            s = s + jnp.maximum(rel, 0.0) * dup(wib_ref[h])
        bits = pltpu.bitcast(s, I32)
        key = bits ^ ((bits >> 31) & 0x7FFFFFFF)
        keys_ref[j] = jnp.where(k0 + col <= qpos, key, INT_MIN)
        return carry

    lax.fori_loop(0, nk, score_tile, 0)

    def count(pred):
        def body(j, acc):
            ind = jnp.where(pred(keys_ref[j], j), 1.0, 0.0)
            part = ind[:, :LANES]
            for c in range(1, tk // LANES):
                part = part + ind[:, c * LANES:(c + 1) * LANES]
            return acc + part
        acc = lax.fori_loop(0, nk, body, jnp.zeros((tq, LANES), F32))
        return jnp.sum(acc, axis=1, keepdims=True)

    def bit_step(t, carry):
        v, cge = carry
        cand_u = v | jnp.left_shift(jnp.int32(1), 31 - t)
        cand = dup(cand_u ^ INT_MIN)
        cnt = count(lambda key, j: key >= cand)
        take = cnt >= topk
        return jnp.where(take, cand_u, v), jnp.where(take, cnt, cge)

    v, cge = lax.fori_loop(0, 32, bit_step,
                           (jnp.zeros((tq, LANES), I32), jnp.zeros((tq, 1), F32)))
    thr = v ^ INT_MIN
    thr_w = dup(thr)

    pstar_ref[...] = jnp.full((tq, LANES), seq, I32)
    surplus = jnp.where((v[:, :1] != 0) & (cge > topk), 1.0, 0.0)

    @pl.when(jnp.max(surplus) > 0.0)
    def _():
        cgt = count(lambda key, j: key > thr_w)
        need = topk - cgt

        def pos_step(t, p):
            cand = p | jnp.left_shift(jnp.int32(1), (seq.bit_length() - 1) - t)
            cw = dup(cand)
            before = count(lambda key, j: (key == thr_w) & (j * tk + col < cw))
            return jnp.where(before < need, cand, p)

        p = lax.fori_loop(0, seq.bit_length(), pos_step, jnp.zeros((tq, LANES), I32))
        pstar_ref[...] = jnp.where(surplus > 0.0, p, seq)

    pstar_w = dup(pstar_ref[...])

    acc_ref[...] = jnp.zeros_like(acc_ref)
    m_ref[...] = jnp.full_like(m_ref, NEG)
    dq = q0 // LANES

    n_tiles = seq // tk
    qlat2d = lambda: qlat_ref[...].reshape(N_HEADS * tq, KV_LORA)

    def kv_tile(j):
        return ckv_ref[pl.ds(pl.multiple_of(jnp.clip(j, 0, n_tiles - 1) * tk, tk), tk), :]

    def logits(j):
        return lax.dot_general(qlat2d(), kv_tile(j), (((1,), (1,)), ((), ())),
                               preferred_element_type=F32)

    s_bufs, p_bufs, alpha_bufs = (s0_ref, s1_ref), (p0_ref, p1_ref), (a0_ref, a1_ref)
    s0_ref[...] = logits(0)
    p1_ref[...] = jnp.zeros(p1_ref.shape, BF16)
    a1_ref[...] = jnp.ones(a1_ref.shape, F32)

    def attn_step(j, cur):
        prv = 1 - cur
        s_ref, p_ref, alpha_ref = s_bufs[cur], p_bufs[cur], alpha_bufs[cur]
        s_bufs[prv][...] = logits(j + 1)

        kpos = j * tk + col
        key = keys_ref[jnp.minimum(j, nk - 1)]
        sel = (kpos <= qpos) & ((key > thr_w) | ((key == thr_w) & (kpos <= pstar_w)))
        maskadd = jnp.where(sel, 0.0, NEG)
        dbase = dq - j * (tk // LANES) + 1
        didx = [jnp.clip(dbase - c, 0, n_tab - 1) for c in range(tk // LANES)]
        for h in range(N_HEADS):
            rows = slice(h * tq, (h + 1) * tq)
            bias = jnp.concatenate([tab_ref[d, h] for d in didx], axis=1)
            s = s_ref[rows, :] + bias + maskadd
            s_ref[rows, :] = s
            m_old = m_ref[rows, :]
            m_new = jnp.maximum(m_old, jnp.max(s, axis=1, keepdims=True))
            alpha_ref[rows, :] = jnp.exp(m_old - m_new)
            m_ref[rows, :] = m_new

        kv_ext = jnp.concatenate([kv_tile(j - 1), jnp.ones((tk, LANES), BF16)], axis=1)
        pv = jnp.dot(p_bufs[prv][...], kv_ext, preferred_element_type=F32)
        acc_ref[...] = alpha_bufs[prv][...] * acc_ref[...] + pv

        for h in range(N_HEADS):
            rows = slice(h * tq, (h + 1) * tq)
            p_ref[rows, :] = jnp.exp(s_ref[rows, :] - m_ref[rows, :]).astype(BF16)

    def attn_pair(jj, carry):
        attn_step(2 * jj, 0)
        attn_step(2 * jj + 1, 1)
        return carry

    lax.fori_loop(0, nk // 2 + 1, attn_pair, 0)

    outs = []
    for h in range(N_HEADS):
        rows = slice(h * tq, (h + 1) * tq)
        inv_l = 1.0 / acc_ref[rows, KV_LORA:]
        o_lat = acc_ref[rows, :KV_LORA] * jnp.concatenate([inv_l] * (KV_LORA // LANES), axis=1)
        outs.append(jnp.dot(o_lat.astype(BF16), wuv_ref[h], preferred_element_type=F32))
    o_ref[...] = jnp.concatenate(outs, axis=1).astype(o_ref.dtype)


def dsa_attention(qlat, qidx, widx, ckv, kidx, tab, w_uv, bsz, seq):
    tq, tk = DSA_TQ, DSA_TK
    nq = seq // tq
    n = bsz * seq
    n_tab = tab.shape[0]
    topk = min(IDX_TOPK_MAX, seq // 4)
    body = functools.partial(_dsa_attn_body, seq=seq, topk=topk, n_tab=n_tab)
    return pl.pallas_call(
        body,
        grid=(bsz, nq),
        in_specs=[pl.BlockSpec((N_HEADS, tq, KV_LORA), lambda b, i: (0, b * nq + i, 0)),
                  pl.BlockSpec((IDX_HEADS, tq, IDX_DIM), lambda b, i: (0, b * nq + i, 0)),
                  pl.BlockSpec((tq, LANES), lambda b, i: (b * nq + i, 0)),
                  pl.BlockSpec((seq, KV_LORA), lambda b, i: (b, 0)),
                  pl.BlockSpec((seq, IDX_DIM), lambda b, i: (b, 0)),
                  _resident(tab.shape, lambda b, i: (0, 0, 0, 0)),
                  _resident((N_HEADS, KV_LORA, HEAD_DIM), lambda b, i: (0, 0, 0))],
        out_specs=pl.BlockSpec((tq, N_HEADS * HEAD_DIM), lambda b, i: (b * nq + i, 0)),
        out_shape=jax.ShapeDtypeStruct((n, N_HEADS * HEAD_DIM), BF16),
        scratch_shapes=[pltpu.VMEM((seq // tk, tq, tk), I32),
                        pltpu.VMEM((N_HEADS * tq, KV_LORA + LANES), F32),
                        pltpu.VMEM((N_HEADS * tq, 1), F32),
                        pltpu.VMEM((IDX_HEADS, tq, LANES), F32),
                        pltpu.VMEM((tq, LANES), I32),
                        pltpu.VMEM((N_HEADS * tq, tk), F32),
                        pltpu.VMEM((N_HEADS * tq, tk), F32),
                        pltpu.VMEM((N_HEADS * tq, tk), BF16),
                        pltpu.VMEM((N_HEADS * tq, tk), BF16),
                        pltpu.VMEM((N_HEADS * tq, 1), F32),
                        pltpu.VMEM((N_HEADS * tq, 1), F32)],
        compiler_params=_cparams("arbitrary", "arbitrary"),
        name="dsa_attention",
    )(qlat, qidx, widx, ckv, kidx, tab, w_uv)


def dsa_bias_buckets(seq):
    far = np.maximum(np.arange(seq), N_BUCKETS // 2).astype(np.float64)
    far_bucket = np.floor(np.log(far / (N_BUCKETS // 2)) / math.log(MAX_DISTANCE / (N_BUCKETS // 2))
                          * (N_BUCKETS - N_BUCKETS // 2))
    unsat = np.nonzero(far_bucket < N_BUCKETS - 1 - N_BUCKETS // 2)[0]
    n_sat = int(unsat[-1]) + 1 if unsat.size else 0
    n_tab = min(seq // LANES + 1, -(-(n_sat + LANES - 1) // LANES) + 2)
    t = np.arange(n_tab)[:, None, None]
    qi = np.arange(LANES)[None, :, None]
    ki = np.arange(LANES)[None, None, :]
    dist = np.maximum(LANES * (t - 1) + qi - ki, 0)
    return _t5_bucket(jnp.asarray(dist, I32))


def _dil_body(q_ref, kp_ref, km_ref, vp_ref, vm_ref, bias_ref, o_ref, lse_ref, *, look):
    nblk = pl.program_id(2)
    tq = q_ref.shape[1]
    ncol = look + tq
    colv = lax.broadcasted_iota(I32, (1, ncol), 1)
    first = jnp.where((nblk == 0) & (colv < look), NEG, 0.0)
    lse_ref[0] = jnp.zeros(lse_ref.shape[1:], F32)
    outs = []
    for h in range(N_HEADS):
        sl = slice(h * HEAD_DIM, (h + 1) * HEAD_DIM)
        q = q_ref[0, :, sl]
        k = jnp.concatenate([kp_ref[0, :, sl], km_ref[0, :, sl]], axis=0)
        v = jnp.concatenate([vp_ref[0, :, sl], vm_ref[0, :, sl]], axis=0)
        s = lax.dot_general(q, k, (((1,), (1,)), ((), ())), preferred_element_type=F32)
        s = s + bias_ref[0, h] + first
        m = jnp.max(s, axis=1, keepdims=True)
        p = jnp.exp(s - m)
        l = jnp.sum(p, axis=1, keepdims=True)
        o = jnp.dot(p.astype(BF16), v, preferred_element_type=F32) * (1.0 / l)
        outs.append(o)
        lse_ref[0, :, h:h + 1] = m + jnp.log(l)
    o_ref[0] = jnp.concatenate(outs, axis=1).astype(o_ref.dtype)


def dilated_branch(qkv, bias, bsz, seq, dil, look):
    hw = N_HEADS * HEAD_DIM
    ls = seq // dil
    tq = bias.shape[2]
    nb = ls // tq
    r = tq // look
    view = qkv.reshape(bsz, ls, dil * 3 * hw)
    prev = lambda part: (lambda b, c, n: (b, jnp.maximum(n * r - 1, 0), c * 3 + part))
    main = lambda part: (lambda b, c, n: (b, n, c * 3 + part))
    o, lse = pl.pallas_call(
        functools.partial(_dil_body, look=look),
        grid=(bsz, dil, nb),
        in_specs=[pl.BlockSpec((1, tq, hw), main(0)),
                  pl.BlockSpec((1, look, hw), prev(1)),
                  pl.BlockSpec((1, tq, hw), main(1)),
                  pl.BlockSpec((1, look, hw), prev(2)),
                  pl.BlockSpec((1, tq, hw), main(2)),
                  _resident(bias.shape, lambda b, c, n: (0, 0, 0, 0))],
        out_specs=[pl.BlockSpec((1, tq, hw), lambda b, c, n: (b, n, c)),
                   pl.BlockSpec((1, tq, LANES), lambda b, c, n: (b, n, c))],
        out_shape=[jax.ShapeDtypeStruct((bsz, ls, dil * hw), BF16),
                   jax.ShapeDtypeStruct((bsz, ls, dil * LANES), F32)],
        compiler_params=_cparams("arbitrary", "arbitrary", "arbitrary"),
        name=f"dilated_d{dil}",
    )(view, view, view, view, view, bias)
    return o.reshape(bsz * seq, hw), lse.reshape(bsz * seq, LANES)


def dilated_bias_buckets(tq, look, dil):
    rel = np.arange(tq)[:, None] + look - np.arange(look + tq)[None, :]
    band = (rel >= 0) & (rel <= look)
    bucket = _t5_bucket(jnp.asarray(dil * np.maximum(rel, 0), I32))
    return jnp.where(jnp.asarray(band), bucket, MASK_BUCKET)[None]


def _post_attn_body(*refs, n_parts, alpha):
    o_refs = refs[:n_parts]
    lse_refs = refs[n_parts:2 * n_parts] if n_parts > 1 else ()
    base = n_parts + len(lse_refs)
    wo_ref, x_ref, g_ref, b_ref, xo_ref, xb_ref = refs[base:base + 6]
    if n_parts == 1:
        o = o_refs[0][...]
    else:
        lses = [r[...] for r in lse_refs]
        mx = functools.reduce(jnp.maximum, lses)
        ws = [jnp.exp(v - mx) for v in lses]
        inv = 1.0 / functools.reduce(lambda a, b: a + b, ws)
        ws = [w * inv for w in ws]
        cols = []
        for h in range(N_HEADS):
            sl = slice(h * HEAD_DIM, (h + 1) * HEAD_DIM)
            acc = None
            for part in range(n_parts):
                term = ws[part][:, h:h + 1] * o_refs[part][:, sl].astype(F32)
                acc = term if acc is None else acc + term
            cols.append(acc)
        o = jnp.concatenate(cols, axis=1).astype(BF16)
    hproj = jnp.dot(o, wo_ref[...], preferred_element_type=F32)
    y = _layer_norm(alpha * x_ref[...] + hproj, g_ref[...], b_ref[...])
    xo_ref[...] = y
    xb_ref[...] = y.astype(BF16)


def post_attention(o_parts, lse_parts, w_o, x, g, b, alpha, tm):
    n, d = x.shape
    hw = N_HEADS * HEAD_DIM
    n_parts = len(o_parts)
    row = lambda width: pl.BlockSpec((tm, width), lambda i: (i, 0))
    in_specs = ([row(hw)] * n_parts + [row(LANES)] * len(lse_parts)
                + [_resident((hw, d), lambda i: (0, 0)), row(d),
                   _resident((1, d), lambda i: (0, 0)), _resident((1, d), lambda i: (0, 0))])
    return pl.pallas_call(
        functools.partial(_post_attn_body, n_parts=n_parts, alpha=alpha),
        grid=(n // tm,),
        in_specs=in_specs,
        out_specs=[row(d), row(d)],
        out_shape=[jax.ShapeDtypeStruct((n, d), F32), jax.ShapeDtypeStruct((n, d), BF16)],
        compiler_params=_cparams("arbitrary"),
        name=f"post_attention_{n_parts}",
    )(*o_parts, *lse_parts, w_o, x, g, b)


def _router_body(x_ref, rwt_ref, rb_ref, comb_ref):
    tm = x_ref.shape[0]
    logits = lax.dot_general(rwt_ref[...], x_ref[...], (((1,), (1,)), ((), ())),
                             preferred_element_type=F32,
                             precision=lax.Precision.HIGHEST)
    rows = [logits[e:e + 1, :] for e in range(N_EXPERTS)]
    mx = functools.reduce(jnp.maximum, rows)
    ex = [jnp.exp(r - mx) for r in rows]
    inv = 1.0 / functools.reduce(lambda a, b: a + b, ex)
    probs = [e * inv for e in ex]
    sel = [probs[e] + rb_ref[e] for e in range(N_EXPERTS)]

    zero_i = jnp.zeros((1, tm), I32)
    best_t = None
    for g in range(N_GROUPS):
        vals = sel[g * EXPERTS_PER_GROUP:(g + 1) * EXPERTS_PER_GROUP]
        prb = probs[g * EXPERTS_PER_GROUP:(g + 1) * EXPERTS_PER_GROUP]
        top, top_i, top_p = vals[0], zero_i, prb[0]
        for j in range(1, EXPERTS_PER_GROUP):
            c = vals[j] > top
            top = jnp.where(c, vals[j], top)
            top_i = jnp.where(c, j, top_i)
            top_p = jnp.where(c, prb[j], top_p)
        sec = jnp.full((1, tm), -jnp.inf, F32)
        sec_i, sec_p = zero_i, jnp.zeros((1, tm), F32)
        for j in range(EXPERTS_PER_GROUP):
            c = (vals[j] > sec) & (top_i != j)
            sec = jnp.where(c, vals[j], sec)
            sec_i = jnp.where(c, j, sec_i)
            sec_p = jnp.where(c, prb[j], sec_p)
        tot = top + sec
        if best_t is None:
            best_t = tot
            e0, e1 = top_i + g * EXPERTS_PER_GROUP, sec_i + g * EXPERTS_PER_GROUP
            p0, p1 = top_p, sec_p
        else:
            c = tot > best_t
            best_t = jnp.where(c, tot, best_t)
            e0 = jnp.where(c, top_i + g * EXPERTS_PER_GROUP, e0)
            e1 = jnp.where(c, sec_i + g * EXPERTS_PER_GROUP, e1)
            p0 = jnp.where(c, top_p, p0)
            p1 = jnp.where(c, sec_p, p1)
    ginv = 1.0 / (p0 + p1)
    g0, g1 = p0 * ginv, p1 * ginv
    comb_rows = [jnp.where(e0 == e, g0, 0.0) + jnp.where(e1 == e, g1, 0.0)
                 for e in range(N_EXPERTS)]
    comb_rows.append(jnp.zeros((LANES - N_EXPERTS, tm), F32))
    comb_ref[...] = jnp.concatenate(comb_rows, axis=0).T


def router(x, router_wt, router_b, tm):
    n, d = x.shape
    return pl.pallas_call(
        _router_body,
        grid=(n // tm,),
        in_specs=[pl.BlockSpec((tm, d), lambda i: (i, 0)),
                  _resident((N_EXPERTS, d), lambda i: (0, 0)),
                  pl.BlockSpec(memory_space=pltpu.SMEM)],
        out_specs=pl.BlockSpec((tm, LANES), lambda i: (i, 0)),
        out_shape=jax.ShapeDtypeStruct((n, LANES), F32),
        compiler_params=_cparams("arbitrary"),
        name="router",
    )(x, router_wt, router_b)


def _moe_body(xb_ref, comb_ref, wg_ref, wu_ref, wd_ref, x_ref, g_ref, b_ref,
              xo_ref, xbo_ref, acc_ref, *, alpha):
    e = pl.program_id(1)

    @pl.when(e == 0)
    def _():
        acc_ref[...] = jnp.zeros_like(acc_ref)

    xb = xb_ref[...]
    gate = jnp.dot(xb, wg_ref[0], preferred_element_type=F32)
    up = jnp.dot(xb, wu_ref[0], preferred_element_type=F32)
    lane = lax.broadcasted_iota(I32, comb_ref.shape, 1)
    c = jnp.sum(jnp.where(lane == e, comb_ref[...], 0.0), axis=1, keepdims=True)
    hidden = (gate * jax.nn.sigmoid(gate)) * up * c
    acc_ref[...] += jnp.dot(hidden.astype(BF16), wd_ref[0], preferred_element_type=F32)

    @pl.when(e == N_EXPERTS - 1)
    def _():
        y = _layer_norm(alpha * x_ref[...] + acc_ref[...], g_ref[...], b_ref[...])
        xo_ref[...] = y
        xbo_ref[...] = y.astype(BF16)


def moe_dense(xb, comb, w_gate, w_up, w_down, x, g, b, alpha, tm):
    n, d = x.shape
    row = lambda width: pl.BlockSpec((tm, width), lambda i, e: (i, 0))
    return pl.pallas_call(
        functools.partial(_moe_body, alpha=alpha),
        grid=(n // tm, N_EXPERTS),
        in_specs=[row(d), row(LANES),
                  pl.BlockSpec((1, d, D_EXPERT), lambda i, e: (e, 0, 0)),
                  pl.BlockSpec((1, d, D_EXPERT), lambda i, e: (e, 0, 0)),
                  pl.BlockSpec((1, D_EXPERT, d), lambda i, e: (e, 0, 0)),
                  row(d),
                  _resident((1, d), lambda i, e: (0, 0)), _resident((1, d), lambda i, e: (0, 0))],
        out_specs=[row(d), row(d)],
        out_shape=[jax.ShapeDtypeStruct((n, d), F32), jax.ShapeDtypeStruct((n, d), BF16)],
        scratch_shapes=[pltpu.VMEM((tm, d), F32)],
        compiler_params=_cparams("arbitrary", "arbitrary"),
        name="moe_dense",
    )(xb, comb, w_gate, w_up, w_down, x, g, b)


def kernel(x, rel_bias, router_w, router_b, a_w_in, a_q_norm, a_kv_norm, a_w_uq, a_w_uq_idx,
           a_w_uk, a_w_uv, a_w_o, b_w_in, b_w_o, ln1_g, ln1_b, ln2_g, ln2_b,
           moe_w_gate, moe_w_up, moe_w_down):
    bsz, seq, d = x.shape
    depth = ln1_g.shape[0]
    n = bsz * seq
    hw = N_HEADS * HEAD_DIM
    alpha = (2 * depth) ** 0.25

    x = x.reshape(n, d)
    xb = x.astype(BF16)
    router_wt = router_w.T

    dsa_tab = bias_expand(dsa_bias_buckets(seq), rel_bias)
    dil_bias = []
    for window, dil in DILATED_PATTERNS:
        look = window // dil
        tq = min(2 * look, seq // dil)
        dil_bias.append(bias_expand(dilated_bias_buckets(tq, look, dil), rel_bias))

    for i in range(depth):
        j = i // 2
        if i % 2 == 0:
            w_in = a_w_in[j]
            kv_end = Q_LORA + KV_LORA
            w_in_p = jnp.concatenate(
                [w_in[:, :kv_end + IDX_DIM], jnp.zeros((d, LANES - IDX_DIM), F32),
                 w_in[:, kv_end + IDX_DIM:], jnp.zeros((d, LANES - IDX_HEADS), F32)],
                axis=1).astype(BF16)
            qlat, qidx, widx, ckv, kidx = dsa_prologue(
                xb, w_in_p, a_q_norm[j][None], a_kv_norm[j][None], a_w_uq[j].astype(BF16),
                a_w_uq_idx[j].astype(BF16), a_w_uk[j].astype(BF16), tm=256)
            o = dsa_attention(qlat, qidx, widx, ckv, kidx, dsa_tab, a_w_uv[j].astype(BF16),
                              bsz, seq)
            o_parts, lse_parts, w_o = [o], [], a_w_o[j]
        else:
            qkv = matmul(xb, b_w_in[j].astype(BF16), BF16, tm=512, tn=1024,
                         scaled_blocks=hw // 1024, scale=HEAD_DIM ** -0.5)
            o_parts, lse_parts = [], []
            for (window, dil), bias in zip(DILATED_PATTERNS, dil_bias):
                o_i, lse_i = dilated_branch(qkv, bias, bsz, seq, dil, window // dil)
                o_parts.append(o_i)
                lse_parts.append(lse_i)
            w_o = b_w_o[j]
        x, xb = post_attention(o_parts, lse_parts, w_o.astype(BF16), x, ln1_g[i][None],
                               ln1_b[i][None], alpha, tm=256)
        comb = router(x, router_wt, router_b, tm=512)
        x, xb = moe_dense(xb, comb, moe_w_gate[i].astype(BF16), moe_w_up[i].astype(BF16),
                          moe_w_down[i].astype(BF16), x, ln2_g[i][None], ln2_b[i][None],
                          alpha, tm=512)
    return x.reshape(bsz, seq, d)
```

```python
import functools
import math

import numpy as np
import jax
import jax.numpy as jnp
from jax import lax
from jax.experimental import pallas as pl
from jax.experimental.pallas import tpu as pltpu

F32 = jnp.float32
BF16 = jnp.bfloat16
I32 = jnp.int32

N_HEADS = 16
HEAD_DIM = 128
Q_LORA = 512
KV_LORA = 256
IDX_HEADS = 16
IDX_DIM = 64
IDX_TOPK_MAX = 256
DILATED_PATTERNS = ((128, 1), (512, 4), (2048, 16))
N_BUCKETS = 32
MAX_DISTANCE = 2048
N_EXPERTS = 16
N_GROUPS = 4
EXPERTS_PER_GROUP = N_EXPERTS // N_GROUPS
D_EXPERT = 512
EPS = 1e-5

LANES = 128
NEG = -1e30
INT_MIN = -(2 ** 31)
MASK_BUCKET = N_BUCKETS
VMEM_LIMIT = 56 * 1024 * 1024


def _cparams(*sem):
    return pltpu.CompilerParams(dimension_semantics=sem, vmem_limit_bytes=VMEM_LIMIT)


def _resident(shape, index_map):
    return pl.BlockSpec(shape, index_map, pipeline_mode=pl.Buffered(1))


def _t5_bucket(dist):
    n = jnp.maximum(dist, 0)
    max_exact = N_BUCKETS // 2
    nf = jnp.maximum(n, max_exact).astype(F32)
    large = max_exact + (jnp.log(nf / max_exact) / math.log(MAX_DISTANCE / max_exact)
                         * (N_BUCKETS - max_exact)).astype(I32)
    large = jnp.minimum(large, N_BUCKETS - 1)
    return jnp.where(n < max_exact, n, large)


def _layer_norm(v, g, b):
    mu = jnp.mean(v, axis=-1, keepdims=True)
    c = v - mu
    var = jnp.mean(c * c, axis=-1, keepdims=True)
    return c * lax.rsqrt(var + EPS) * g + b


def _rms_norm(v, g):
    return v * lax.rsqrt(jnp.mean(v * v, axis=-1, keepdims=True) + EPS) * g


def _bias_expand_body(bucket_ref, rb_ref, o_ref):
    bk = bucket_ref[0]
    for h in range(N_HEADS):
        acc = jnp.where(bk == MASK_BUCKET, NEG, 0.0).astype(F32)
        for b in range(N_BUCKETS):
            acc = jnp.where(bk == b, rb_ref[b, h], acc)
        o_ref[0, h] = acc


def bias_expand(bucket, rel_bias):
    n, r, c = bucket.shape
    return pl.pallas_call(
        _bias_expand_body,
        grid=(n,),
        in_specs=[pl.BlockSpec((1, r, c), lambda i: (i, 0, 0)),
                  pl.BlockSpec(memory_space=pltpu.SMEM)],
        out_specs=pl.BlockSpec((1, N_HEADS, r, c), lambda i: (i, 0, 0, 0)),
        out_shape=jax.ShapeDtypeStruct((n, N_HEADS, r, c), F32),
        compiler_params=_cparams("arbitrary"),
        name="bias_expand",
    )(bucket, rel_bias)


def _matmul_body(x_ref, w_ref, o_ref, *, scaled_blocks, scale):
    acc = jnp.dot(x_ref[...], w_ref[...], preferred_element_type=F32)
    if scaled_blocks:
        acc = acc * jnp.where(pl.program_id(0) < scaled_blocks, scale, 1.0).astype(F32)
    o_ref[...] = acc.astype(o_ref.dtype)


def matmul(x, w, out_dtype, tm, tn, scaled_blocks=0, scale=1.0):
    m, k = x.shape
    n = w.shape[1]
    return pl.pallas_call(
        functools.partial(_matmul_body, scaled_blocks=scaled_blocks, scale=scale),
        grid=(n // tn, m // tm),
        in_specs=[pl.BlockSpec((tm, k), lambda j, i: (i, 0)),
                  pl.BlockSpec((k, tn), lambda j, i: (0, j))],
        out_specs=pl.BlockSpec((tm, tn), lambda j, i: (i, j)),
        out_shape=jax.ShapeDtypeStruct((m, n), out_dtype),
        compiler_params=_cparams("arbitrary", "arbitrary"),
        name="matmul",
    )(x, w)


def _dsa_pro_body(x_ref, win_ref, qn_ref, kvn_ref, wuq_ref, wuqi_ref, wuk_ref,
                  qlat_ref, qidx_ref, widx_ref, ckv_ref, kidx_ref):
    proj = jnp.dot(x_ref[...], win_ref[...], preferred_element_type=F32)
    c_q = _rms_norm(proj[:, :Q_LORA], qn_ref[...]).astype(BF16)
    c_kv = _rms_norm(proj[:, Q_LORA:Q_LORA + KV_LORA], kvn_ref[...])
    ckv_ref[...] = c_kv.astype(BF16)
    kidx_ref[...] = proj[:, 768:768 + IDX_DIM].astype(BF16)
    widx_ref[...] = proj[:, 896:1024] * (IDX_HEADS ** -0.5)
    q = jnp.dot(c_q, wuq_ref[...], preferred_element_type=F32)
    for h in range(N_HEADS):
        qh = q[:, h * HEAD_DIM:(h + 1) * HEAD_DIM].astype(BF16)
        ql = jnp.dot(qh, wuk_ref[h], preferred_element_type=F32) * (HEAD_DIM ** -0.5)
        qlat_ref[h] = ql.astype(BF16)
    qi = jnp.dot(c_q, wuqi_ref[...], preferred_element_type=F32) * (IDX_DIM ** -0.5)
    for h in range(IDX_HEADS):
        qidx_ref[h] = qi[:, h * IDX_DIM:(h + 1) * IDX_DIM].astype(BF16)


def dsa_prologue(xb, w_in_p, q_norm, kv_norm, w_uq, w_uq_idx, w_uk, tm):
    n, d = xb.shape
    hw = N_HEADS * HEAD_DIM
    return pl.pallas_call(
        _dsa_pro_body,
        grid=(n // tm,),
        in_specs=[pl.BlockSpec((tm, d), lambda i: (i, 0)),
                  _resident((d, 1024), lambda i: (0, 0)),
                  _resident((1, Q_LORA), lambda i: (0, 0)),
                  _resident((1, KV_LORA), lambda i: (0, 0)),
                  _resident((Q_LORA, hw), lambda i: (0, 0)),
                  _resident((Q_LORA, IDX_HEADS * IDX_DIM), lambda i: (0, 0)),
                  _resident((N_HEADS, HEAD_DIM, KV_LORA), lambda i: (0, 0, 0))],
        out_specs=[pl.BlockSpec((N_HEADS, tm, KV_LORA), lambda i: (0, i, 0)),
                   pl.BlockSpec((IDX_HEADS, tm, IDX_DIM), lambda i: (0, i, 0)),
                   pl.BlockSpec((tm, LANES), lambda i: (i, 0)),
                   pl.BlockSpec((tm, KV_LORA), lambda i: (i, 0)),
                   pl.BlockSpec((tm, IDX_DIM), lambda i: (i, 0))],
        out_shape=[jax.ShapeDtypeStruct((N_HEADS, n, KV_LORA), BF16),
                   jax.ShapeDtypeStruct((IDX_HEADS, n, IDX_DIM), BF16),
                   jax.ShapeDtypeStruct((n, LANES), F32),
                   jax.ShapeDtypeStruct((n, KV_LORA), BF16),
                   jax.ShapeDtypeStruct((n, IDX_DIM), BF16)],
        compiler_params=_cparams("arbitrary"),
        name="dsa_prologue",
    )(xb, w_in_p, q_norm, kv_norm, w_uq, w_uq_idx, w_uk)


DSA_TQ = 128
DSA_TK = 256


def _dsa_attn_body(qlat_ref, qidx_ref, widx_ref, ckv_ref, kidx_ref, tab_ref, wuv_ref, o_ref,
                   keys_ref, acc_ref, m_ref, wib_ref, pstar_ref,
                   s0_ref, s1_ref, p0_ref, p1_ref, a0_ref, a1_ref,
                   *, seq, topk, n_tab):
    tq, tk = DSA_TQ, DSA_TK
    i = pl.program_id(1)
    q0 = i * tq
    nk = (q0 + tq + tk - 1) // tk
    qpos = q0 + lax.broadcasted_iota(I32, (tq, tk), 0)
    col = lax.broadcasted_iota(I32, (tq, tk), 1)

    def dup(v):
        return jnp.concatenate([v] * (tk // LANES), axis=1)

    for h in range(IDX_HEADS):
        wib_ref[h] = jnp.broadcast_to(widx_ref[:, h:h + 1], (tq, LANES))

    def score_tile(j, carry):
        k0 = pl.multiple_of(j * tk, tk)
        kt = kidx_ref[pl.ds(k0, tk), :]
        s = jnp.zeros((tq, tk), F32)
        for h in range(IDX_HEADS):
            rel = lax.dot_general(qidx_ref[h], kt, (((1,), (1,)), ((), ())),
                                  preferred_element_type=F32)
            s = s + jnp.maximum(rel, 0.0) * dup(wib_ref[h])
        bits = pltpu.bitcast(s, I32)
        key = bits ^ ((bits >> 31) & 0x7FFFFFFF)
        keys_ref[j] = jnp.where(k0 + col <= qpos, key, INT_MIN)
        return carry

    lax.fori_loop(0, nk, score_tile, 0)

    def count(pred):
        def body(j, acc):
            ind = jnp.where(pred(keys_ref[j], j), 1.0, 0.0)
            part = ind[:, :LANES]
            for c in range(1, tk // LANES):
                part = part + ind[:, c * LANES:(c + 1) * LANES]
            return acc + part
        acc = lax.fori_loop(0, nk, body, jnp.zeros((tq, LANES), F32))
        return jnp.sum(acc, axis=1, keepdims=True)

    def bit_step(t, carry):
        v, cge = carry
        cand_u = v | jnp.left_shift(jnp.int32(1), 31 - t)
        cand = dup(cand_u ^ INT_MIN)
        cnt = count(lambda key, j: key >= cand)
        take = cnt >= topk
        return jnp.where(take, cand_u, v), jnp.where(take, cnt, cge)

    v, cge = lax.fori_loop(0, 32, bit_step,
                           (jnp.zeros((tq, LANES), I32), jnp.zeros((tq, 1), F32)))
    thr = v ^ INT_MIN
    thr_w = dup(thr)

    pstar_ref[...] = jnp.full((tq, LANES), seq, I32)
    surplus = jnp.where((v[:, :1] != 0) & (cge > topk), 1.0, 0.0)

    @pl.when(jnp.max(surplus) > 0.0)
    def _():
        cgt = count(lambda key, j: key > thr_w)
        need = topk - cgt

        def pos_step(t, p):
            cand = p | jnp.left_shift(jnp.int32(1), (seq.bit_length() - 1) - t)
            cw = dup(cand)
            before = count(lambda key, j: (key == thr_w) & (j * tk + col < cw))
            return jnp.where(before < need, cand, p)

        p = lax.fori_loop(0, seq.bit_length(), pos_step, jnp.zeros((tq, LANES), I32))
        pstar_ref[...] = jnp.where(surplus > 0.0, p, seq)

    pstar_w = dup(pstar_ref[...])

    acc_ref[...] = jnp.zeros_like(acc_ref)
    m_ref[...] = jnp.full_like(m_ref, NEG)
    dq = q0 // LANES

    n_tiles = seq // tk
    qlat2d = lambda: qlat_ref[...].reshape(N_HEADS * tq, KV_LORA)

    def kv_tile(j):
        return ckv_ref[pl.ds(pl.multiple_of(jnp.clip(j, 0, n_tiles - 1) * tk, tk), tk), :]

    def logits(j):
        return lax.dot_general(qlat2d(), kv_tile(j), (((1,), (1,)), ((), ())),
                               preferred_element_type=F32)

    s_bufs, p_bufs, alpha_bufs = (s0_ref, s1_ref), (p0_ref, p1_ref), (a0_ref, a1_ref)
    s0_ref[...] = logits(0)
    p1_ref[...] = jnp.zeros(p1_ref.shape, BF16)
    a1_ref[...] = jnp.ones(a1_ref.shape, F32)

    def attn_step(j, cur):
        prv = 1 - cur
        s_ref, p_ref, alpha_ref = s_bufs[cur], p_bufs[cur], alpha_bufs[cur]
        s_bufs[prv][...] = logits(j + 1)

        kpos = j * tk + col
        key = keys_ref[jnp.minimum(j, nk - 1)]
        sel = (kpos <= qpos) & ((key > thr_w) | ((key == thr_w) & (kpos <= pstar_w)))
        maskadd = jnp.where(sel, 0.0, NEG)
        dbase = dq - j * (tk // LANES) + 1
        didx = [jnp.clip(dbase - c, 0, n_tab - 1) for c in range(tk // LANES)]
        for h in range(N_HEADS):
            rows = slice(h * tq, (h + 1) * tq)
            bias = jnp.concatenate([tab_ref[d, h] for d in didx], axis=1)
            s = s_ref[rows, :] + bias + maskadd
            s_ref[rows, :] = s
            m_old = m_ref[rows, :]
            m_new = jnp.maximum(m_old, jnp.max(s, axis=1, keepdims=True))
            alpha_ref[rows, :] = jnp.exp(m_old - m_new)
            m_ref[rows, :] = m_new

        kv_ext = jnp.concatenate([kv_tile(j - 1), jnp.ones((tk, LANES), BF16)], axis=1)
        pv = jnp.dot(p_bufs[prv][...], kv_ext, preferred_element_type=F32)
        acc_ref[...] = alpha_bufs[prv][...] * acc_ref[...] + pv

        for h in range(N_HEADS):
            rows = slice(h * tq, (h + 1) * tq)
            p_ref[rows, :] = jnp.exp(s_ref[rows, :] - m_ref[rows, :]).astype(BF16)

    def attn_pair(jj, carry):
        attn_step(2 * jj, 0)
        attn_step(2 * jj + 1, 1)
        return carry

    lax.fori_loop(0, nk // 2 + 1, attn_pair, 0)

    outs = []
    for h in range(N_HEADS):
        rows = slice(h * tq, (h + 1) * tq)
        inv_l = 1.0 / acc_ref[rows, KV_LORA:]
        o_lat = acc_ref[rows, :KV_LORA] * jnp.concatenate([inv_l] * (KV_LORA // LANES), axis=1)
        outs.append(jnp.dot(o_lat.astype(BF16), wuv_ref[h], preferred_element_type=F32))
    o_ref[...] = jnp.concatenate(outs, axis=1).astype(o_ref.dtype)


def dsa_attention(qlat, qidx, widx, ckv, kidx, tab, w_uv, bsz, seq):
    tq, tk = DSA_TQ, DSA_TK
    nq = seq // tq
    n = bsz * seq
    n_tab = tab.shape[0]
    topk = min(IDX_TOPK_MAX, seq // 4)
    body = functools.partial(_dsa_attn_body, seq=seq, topk=topk, n_tab=n_tab)
    return pl.pallas_call(
        body,
        grid=(bsz, nq),
        in_specs=[pl.BlockSpec((N_HEADS, tq, KV_LORA), lambda b, i: (0, b * nq + i, 0)),
                  pl.BlockSpec((IDX_HEADS, tq, IDX_DIM), lambda b, i: (0, b * nq + i, 0)),
                  pl.BlockSpec((tq, LANES), lambda b, i: (b * nq + i, 0)),
                  pl.BlockSpec((seq, KV_LORA), lambda b, i: (b, 0)),
                  pl.BlockSpec((seq, IDX_DIM), lambda b, i: (b, 0)),
                  _resident(tab.shape, lambda b, i: (0, 0, 0, 0)),
                  _resident((N_HEADS, KV_LORA, HEAD_DIM), lambda b, i: (0, 0, 0))],
        out_specs=pl.BlockSpec((tq, N_HEADS * HEAD_DIM), lambda b, i: (b * nq + i, 0)),
        out_shape=jax.ShapeDtypeStruct((n, N_HEADS * HEAD_DIM), BF16),
        scratch_shapes=[pltpu.VMEM((seq // tk, tq, tk), I32),
                        pltpu.VMEM((N_HEADS * tq, KV_LORA + LANES), F32),
                        pltpu.VMEM((N_HEADS * tq, 1), F32),
                        pltpu.VMEM((IDX_HEADS, tq, LANES), F32),
                        pltpu.VMEM((tq, LANES), I32),
                        pltpu.VMEM((N_HEADS * tq, tk), F32),
                        pltpu.VMEM((N_HEADS * tq, tk), F32),
                        pltpu.VMEM((N_HEADS * tq, tk), BF16),
                        pltpu.VMEM((N_HEADS * tq, tk), BF16),
                        pltpu.VMEM((N_HEADS * tq, 1), F32),
                        pltpu.VMEM((N_HEADS * tq, 1), F32)],
        compiler_params=_cparams("arbitrary", "arbitrary"),
        name="dsa_attention",
    )(qlat, qidx, widx, ckv, kidx, tab, w_uv)


def dsa_bias_buckets(seq):
    far = np.maximum(np.arange(seq), N_BUCKETS // 2).astype(np.float64)
    far_bucket = np.floor(np.log(far / (N_BUCKETS // 2)) / math.log(MAX_DISTANCE / (N_BUCKETS // 2))
                          * (N_BUCKETS - N_BUCKETS // 2))
    unsat = np.nonzero(far_bucket < N_BUCKETS - 1 - N_BUCKETS // 2)[0]
    n_sat = int(unsat[-1]) + 1 if unsat.size else 0
    n_tab = min(seq // LANES + 1, -(-(n_sat + LANES - 1) // LANES) + 2)
    t = np.arange(n_tab)[:, None, None]
    qi = np.arange(LANES)[None, :, None]
    ki = np.arange(LANES)[None, None, :]
    dist = np.maximum(LANES * (t - 1) + qi - ki, 0)
    return _t5_bucket(jnp.asarray(dist, I32))


def _dil_body(q_ref, kp_ref, km_ref, vp_ref, vm_ref, bias_ref, o_ref, lse_ref, *, look):
    nblk = pl.program_id(2)
    tq = q_ref.shape[1]
    ncol = look + tq
    colv = lax.broadcasted_iota(I32, (1, ncol), 1)
    first = jnp.where((nblk == 0) & (colv < look), NEG, 0.0)
    lse_ref[0] = jnp.zeros(lse_ref.shape[1:], F32)
    outs = []
    for h in range(N_HEADS):
        sl = slice(h * HEAD_DIM, (h + 1) * HEAD_DIM)
        q = q_ref[0, :, sl]
        k = jnp.concatenate([kp_ref[0, :, sl], km_ref[0, :, sl]], axis=0)
        v = jnp.concatenate([vp_ref[0, :, sl], vm_ref[0, :, sl]], axis=0)
        s = lax.dot_general(q, k, (((1,), (1,)), ((), ())), preferred_element_type=F32)
        s = s + bias_ref[0, h] + first
        m = jnp.max(s, axis=1, keepdims=True)
        p = jnp.exp(s - m)
        l = jnp.sum(p, axis=1, keepdims=True)
        o = jnp.dot(p.astype(BF16), v, preferred_element_type=F32) * (1.0 / l)
        outs.append(o)
        lse_ref[0, :, h:h + 1] = m + jnp.log(l)
    o_ref[0] = jnp.concatenate(outs, axis=1).astype(o_ref.dtype)


def dilated_branch(qkv, bias, bsz, seq, dil, look):
    hw = N_HEADS * HEAD_DIM
    ls = seq // dil
    tq = bias.shape[2]
    nb = ls // tq
    r = tq // look
    view = qkv.reshape(bsz, ls, dil * 3 * hw)
    prev = lambda part: (lambda b, c, n: (b, jnp.maximum(n * r - 1, 0), c * 3 + part))
    main = lambda part: (lambda b, c, n: (b, n, c * 3 + part))
    o, lse = pl.pallas_call(
        functools.partial(_dil_body, look=look),
        grid=(bsz, dil, nb),
        in_specs=[pl.BlockSpec((1, tq, hw), main(0)),
                  pl.BlockSpec((1, look, hw), prev(1)),
                  pl.BlockSpec((1, tq, hw), main(1)),
                  pl.BlockSpec((1, look, hw), prev(2)),
                  pl.BlockSpec((1, tq, hw), main(2)),
                  _resident(bias.shape, lambda b, c, n: (0, 0, 0, 0))],
        out_specs=[pl.BlockSpec((1, tq, hw), lambda b, c, n: (b, n, c)),
                   pl.BlockSpec((1, tq, LANES), lambda b, c, n: (b, n, c))],
        out_shape=[jax.ShapeDtypeStruct((bsz, ls, dil * hw), BF16),
                   jax.ShapeDtypeStruct((bsz, ls, dil * LANES), F32)],
        compiler_params=_cparams("arbitrary", "arbitrary", "arbitrary"),
        name=f"dilated_d{dil}",
    )(view, view, view, view, view, bias)
    return o.reshape(bsz * seq, hw), lse.reshape(bsz * seq, LANES)


def dilated_bias_buckets(tq, look, dil):
    rel = np.arange(tq)[:, None] + look - np.arange(look + tq)[None, :]
    band = (rel >= 0) & (rel <= look)
    bucket = _t5_bucket(jnp.asarray(dil * np.maximum(rel, 0), I32))
    return jnp.where(jnp.asarray(band), bucket, MASK_BUCKET)[None]


def _post_attn_body(*refs, n_parts, alpha):
    o_refs = refs[:n_parts]
    lse_refs = refs[n_parts:2 * n_parts] if n_parts > 1 else ()
    base = n_parts + len(lse_refs)
    wo_ref, x_ref, g_ref, b_ref, xo_ref, xb_ref = refs[base:base + 6]
    if n_parts == 1:
        o = o_refs[0][...]
    else:
        lses = [r[...] for r in lse_refs]
        mx = functools.reduce(jnp.maximum, lses)
        ws = [jnp.exp(v - mx) for v in lses]
        inv = 1.0 / functools.reduce(lambda a, b: a + b, ws)
        ws = [w * inv for w in ws]
        cols = []
        for h in range(N_HEADS):
            sl = slice(h * HEAD_DIM, (h + 1) * HEAD_DIM)
            acc = None
            for part in range(n_parts):
                term = ws[part][:, h:h + 1] * o_refs[part][:, sl].astype(F32)
                acc = term if acc is None else acc + term
            cols.append(acc)
        o = jnp.concatenate(cols, axis=1).astype(BF16)
    hproj = jnp.dot(o, wo_ref[...], preferred_element_type=F32)
    y = _layer_norm(alpha * x_ref[...] + hproj, g_ref[...], b_ref[...])
    xo_ref[...] = y
    xb_ref[...] = y.astype(BF16)


def post_attention(o_parts, lse_parts, w_o, x, g, b, alpha, tm):
    n, d = x.shape
    hw = N_HEADS * HEAD_DIM
    n_parts = len(o_parts)
    row = lambda width: pl.BlockSpec((tm, width), lambda i: (i, 0))
    in_specs = ([row(hw)] * n_parts + [row(LANES)] * len(lse_parts)
                + [_resident((hw, d), lambda i: (0, 0)), row(d),
                   _resident((1, d), lambda i: (0, 0)), _resident((1, d), lambda i: (0, 0))])
    return pl.pallas_call(
        functools.partial(_post_attn_body, n_parts=n_parts, alpha=alpha),
        grid=(n // tm,),
        in_specs=in_specs,
        out_specs=[row(d), row(d)],
        out_shape=[jax.ShapeDtypeStruct((n, d), F32), jax.ShapeDtypeStruct((n, d), BF16)],
        compiler_params=_cparams("arbitrary"),
        name=f"post_attention_{n_parts}",
    )(*o_parts, *lse_parts, w_o, x, g, b)


def _router_body(x_ref, rwt_ref, rb_ref, route_ref, gates_ref, counts_ref, carry_ref):
    tm = x_ref.shape[0]

    @pl.when(pl.program_id(0) == 0)
    def _():
        carry_ref[...] = jnp.zeros_like(carry_ref)

    logits = lax.dot_general(rwt_ref[...], x_ref[...], (((1,), (1,)), ((), ())),
                             preferred_element_type=F32,
                             precision=lax.Precision.HIGHEST)
    rows = [logits[e:e + 1, :] for e in range(N_EXPERTS)]
    mx = functools.reduce(jnp.maximum, rows)
    ex = [jnp.exp(r - mx) for r in rows]
    inv = 1.0 / functools.reduce(lambda a, b: a + b, ex)
    probs = [e * inv for e in ex]
    sel = [probs[e] + rb_ref[e] for e in range(N_EXPERTS)]

    zero_i = jnp.zeros((1, tm), I32)
    best_t = None
    for g in range(N_GROUPS):
        vals = sel[g * EXPERTS_PER_GROUP:(g + 1) * EXPERTS_PER_GROUP]
        prb = probs[g * EXPERTS_PER_GROUP:(g + 1) * EXPERTS_PER_GROUP]
        top, top_i, top_p = vals[0], zero_i, prb[0]
        for j in range(1, EXPERTS_PER_GROUP):
            c = vals[j] > top
            top = jnp.where(c, vals[j], top)
            top_i = jnp.where(c, j, top_i)
            top_p = jnp.where(c, prb[j], top_p)
        sec = jnp.full((1, tm), -jnp.inf, F32)
        sec_i, sec_p = zero_i, jnp.zeros((1, tm), F32)
        for j in range(EXPERTS_PER_GROUP):
            c = (vals[j] > sec) & (top_i != j)
            sec = jnp.where(c, vals[j], sec)
            sec_i = jnp.where(c, j, sec_i)
            sec_p = jnp.where(c, prb[j], sec_p)
        tot = top + sec
        if best_t is None:
            best_t = tot
            e0, e1 = top_i + g * EXPERTS_PER_GROUP, sec_i + g * EXPERTS_PER_GROUP
            p0, p1 = top_p, sec_p
        else:
            c = tot > best_t
            best_t = jnp.where(c, tot, best_t)
            e0 = jnp.where(c, top_i + g * EXPERTS_PER_GROUP, e0)
            e1 = jnp.where(c, sec_i + g * EXPERTS_PER_GROUP, e1)
            p0 = jnp.where(c, top_p, p0)
            p1 = jnp.where(c, sec_p, p1)
    ginv = 1.0 / (p0 + p1)
    g0, g1 = p0 * ginv, p1 * ginv
    gates_ref[...] = jnp.concatenate([g0, g1, jnp.zeros((LANES - 2, tm), F32)], axis=0).T

    assign = jnp.concatenate([jnp.where((e0 == e) | (e1 == e), 1.0, 0.0)
                              for e in range(N_EXPERTS)], axis=0).astype(BF16)
    upper = jnp.where(lax.broadcasted_iota(I32, (tm, tm), 0)
                      <= lax.broadcasted_iota(I32, (tm, tm), 1), 1.0, 0.0).astype(BF16)
    prefix = jnp.dot(assign, upper, preferred_element_type=F32) + carry_ref[...]
    rank0 = jnp.zeros((1, tm), F32)
    rank1 = jnp.zeros((1, tm), F32)
    for e in range(N_EXPERTS):
        rank0 = jnp.where(e0 == e, prefix[e:e + 1, :], rank0)
        rank1 = jnp.where(e1 == e, prefix[e:e + 1, :], rank1)
    total = prefix[:, tm - 1:tm]
    carry_ref[...] = total
    counts_ref[...] = jnp.broadcast_to(total, counts_ref.shape).astype(I32)
    route_ref[...] = jnp.concatenate(
        [e0, e1, (rank0 - 1.0).astype(I32), (rank1 - 1.0).astype(I32), jnp.zeros((4, tm), I32)],
        axis=0)


def router(x, router_wt, router_b, tm):
    n, d = x.shape
    return pl.pallas_call(
        _router_body,
        grid=(n // tm,),
        in_specs=[pl.BlockSpec((tm, d), lambda i: (i, 0)),
                  _resident((N_EXPERTS, d), lambda i: (0, 0)),
                  pl.BlockSpec(memory_space=pltpu.SMEM)],
        out_specs=[pl.BlockSpec((8, tm), lambda i: (0, i)),
                   pl.BlockSpec((tm, LANES), lambda i: (i, 0)),
                   pl.BlockSpec((N_EXPERTS, LANES), lambda i: (0, 0))],
        out_shape=[jax.ShapeDtypeStruct((8, n), I32),
                   jax.ShapeDtypeStruct((n, LANES), F32),
                   jax.ShapeDtypeStruct((N_EXPERTS, LANES), I32)],
        scratch_shapes=[pltpu.VMEM((N_EXPERTS, 1), F32)],
        compiler_params=_cparams("arbitrary"),
        name="router",
    )(x, router_wt, router_b)


EXPERT_TILE = 512
COMBINE_TILE = 256


def _row_copy(src_hbm, row, dst, dst_row, sem):
    return pltpu.make_async_copy(src_hbm.at[pl.ds(row, 1), :], dst.at[pl.ds(dst_row, 1), :], sem)


def _experts_body(te_ref, nu_ref, src_next_ref, src_first_ref, x_hbm, wg_ref, wu_ref, wd_ref,
                  ys_ref, xbuf, sem, wgb, wub, wdb):
    i = pl.program_id(0)
    n_used = nu_ref[0]
    slot = i % 2
    tmx = ys_ref.shape[0]

    def issue(src_ref, dst_slot):
        def body(r, c):
            _row_copy(x_hbm, src_ref[0, 0, r], xbuf.at[dst_slot], r, sem.at[dst_slot]).start()
            return c
        lax.fori_loop(0, tmx, body, 0, unroll=8)

    @pl.when(i == 0)
    def _():
        issue(src_first_ref, 0)

    @pl.when(i + 1 < n_used)
    def _():
        issue(src_next_ref, 1 - slot)

    @pl.when(i < n_used)
    def _():
        def wait_row(r, c):
            _row_copy(x_hbm, 0, xbuf.at[slot], r, sem.at[slot]).wait()
            return c
        lax.fori_loop(0, tmx, wait_row, 0, unroll=8)

        @pl.when((i == 0) | (te_ref[i] != te_ref[jnp.maximum(i - 1, 0)]))
        def _():
            wgb[...] = wg_ref[0].astype(BF16)
            wub[...] = wu_ref[0].astype(BF16)
            wdb[...] = wd_ref[0].astype(BF16)

        xb = xbuf[slot].astype(BF16)
        gate = jnp.dot(xb, wgb[...], preferred_element_type=F32)
        up = jnp.dot(xb, wub[...], preferred_element_type=F32)
        hidden = (gate * jax.nn.sigmoid(gate)) * up
        ys_ref[...] = jnp.dot(hidden.astype(BF16), wdb[...], preferred_element_type=F32)

    @pl.when(i >= n_used)
    def _():
        ys_ref[...] = jnp.zeros_like(ys_ref)


def moe_experts(x, src_tiles, tile_expert, n_used, w_gate, w_up, w_down):
    n, d = x.shape
    n_tiles = src_tiles.shape[0]
    tmx = EXPERT_TILE
    smem_tile = lambda index_map: pl.BlockSpec((1, 1, tmx), index_map, memory_space=pltpu.SMEM)
    by_expert = lambda shape: pl.BlockSpec(shape, lambda i, te, nu: (te[i], 0, 0))
    grid_spec = pltpu.PrefetchScalarGridSpec(
        num_scalar_prefetch=2,
        grid=(n_tiles,),
        in_specs=[smem_tile(lambda i, te, nu: (jnp.minimum(i + 1, n_tiles - 1), 0, 0)),
                  smem_tile(lambda i, te, nu: (0, 0, 0)),
                  pl.BlockSpec(memory_space=pl.ANY),
                  by_expert((1, d, D_EXPERT)), by_expert((1, d, D_EXPERT)),
                  by_expert((1, D_EXPERT, d))],
        out_specs=pl.BlockSpec((tmx, d), lambda i, te, nu: (i, 0)),
        scratch_shapes=[pltpu.VMEM((2, tmx, d), F32),
                        pltpu.SemaphoreType.DMA((2,)),
                        pltpu.VMEM((d, D_EXPERT), BF16),
                        pltpu.VMEM((d, D_EXPERT), BF16),
                        pltpu.VMEM((D_EXPERT, d), BF16)])
    return pl.pallas_call(
        _experts_body,
        grid_spec=grid_spec,
        out_shape=jax.ShapeDtypeStruct((n_tiles * tmx, d), F32),
        compiler_params=_cparams("arbitrary"),
        name="moe_experts",
    )(tile_expert, n_used, src_tiles, src_tiles, x, w_gate, w_up, w_down)


def _combine_body(pos_next_ref, pos_first_ref, ys_hbm, gates_ref, x_ref, g_ref, b_ref,
                  xo_ref, xbo_ref, ybuf, sem, *, alpha):
    i = pl.program_id(0)
    n_steps = pl.num_programs(0)
    slot = i % 2
    tm = x_ref.shape[0]

    def issue(pos_ref, dst_slot):
        for k in range(2):
            def body(r, c):
                _row_copy(ys_hbm, pos_ref[0, k, r], ybuf.at[dst_slot, k], r,
                          sem.at[dst_slot]).start()
                return c
            lax.fori_loop(0, tm, body, 0, unroll=8)

    @pl.when(i == 0)
    def _():
        issue(pos_first_ref, 0)

    @pl.when(i + 1 < n_steps)
    def _():
        issue(pos_next_ref, 1 - slot)

    def wait_row(r, c):
        for k in range(2):
            _row_copy(ys_hbm, 0, ybuf.at[slot, k], r, sem.at[slot]).wait()
        return c
    lax.fori_loop(0, tm, wait_row, 0, unroll=8)

    gates = gates_ref[...]
    y = gates[:, 0:1] * ybuf[slot, 0] + gates[:, 1:2] * ybuf[slot, 1]
    out = _layer_norm(alpha * x_ref[...] + y, g_ref[...], b_ref[...])
    xo_ref[...] = out
    xbo_ref[...] = out.astype(BF16)


def moe_combine(ys, pos_tiles, gates, x, g, b, alpha):
    n, d = x.shape
    tm = COMBINE_TILE
    n_steps = n // tm
    smem_tile = lambda index_map: pl.BlockSpec((1, 2, tm), index_map, memory_space=pltpu.SMEM)
    row = lambda width: pl.BlockSpec((tm, width), lambda i: (i, 0))
    return pl.pallas_call(
        functools.partial(_combine_body, alpha=alpha),
        grid=(n_steps,),
        in_specs=[smem_tile(lambda i: (jnp.minimum(i + 1, n_steps - 1), 0, 0)),
                  smem_tile(lambda i: (0, 0, 0)),
                  pl.BlockSpec(memory_space=pl.ANY),
                  row(LANES), row(d),
                  _resident((1, d), lambda i: (0, 0)), _resident((1, d), lambda i: (0, 0))],
        out_specs=[row(d), row(d)],
        out_shape=[jax.ShapeDtypeStruct((n, d), F32), jax.ShapeDtypeStruct((n, d), BF16)],
        scratch_shapes=[pltpu.VMEM((2, 2, tm, d), F32), pltpu.SemaphoreType.DMA((2,))],
        compiler_params=_cparams("arbitrary"),
        name="moe_combine",
    )(pos_tiles, pos_tiles, ys, gates, x, g, b)


def moe_layer(x, router_wt, router_b, w_gate, w_up, w_down, g, b, alpha):
    n, d = x.shape
    tmx = EXPERT_TILE
    n_tiles = (2 * n) // tmx + N_EXPERTS
    route, gates, counts = router(x, router_wt, router_b, tm=512)
    counts = counts[:, 0]
    tiles_e = (counts + tmx - 1) // tmx
    tile_end = jnp.cumsum(tiles_e)
    row_off = (tile_end - tiles_e) * tmx
    pos0 = row_off[route[0]] + route[2]
    pos1 = row_off[route[1]] + route[3]
    tok = jnp.arange(n, dtype=I32)
    src = jnp.zeros((n_tiles * tmx,), I32).at[pos0].set(tok).at[pos1].set(tok)
    tile_expert = jnp.minimum(
        jnp.searchsorted(tile_end, jnp.arange(n_tiles, dtype=I32), side="right"),
        N_EXPERTS - 1).astype(I32)
    n_used = tile_end[-1:].astype(I32)
    ys = moe_experts(x, src.reshape(n_tiles, 1, tmx), tile_expert, n_used, w_gate, w_up, w_down)
    tm = COMBINE_TILE
    pos_tiles = jnp.stack([pos0, pos1]).reshape(2, n // tm, tm).transpose(1, 0, 2)
    return moe_combine(ys, pos_tiles, gates, x, g, b, alpha)


def kernel(x, rel_bias, router_w, router_b, a_w_in, a_q_norm, a_kv_norm, a_w_uq, a_w_uq_idx,
           a_w_uk, a_w_uv, a_w_o, b_w_in, b_w_o, ln1_g, ln1_b, ln2_g, ln2_b,
           moe_w_gate, moe_w_up, moe_w_down):
    bsz, seq, d = x.shape
    depth = ln1_g.shape[0]
    n = bsz * seq
    hw = N_HEADS * HEAD_DIM
    alpha = (2 * depth) ** 0.25

    x = x.reshape(n, d)
    xb = x.astype(BF16)
    router_wt = router_w.T

    dsa_tab = bias_expand(dsa_bias_buckets(seq), rel_bias)
    dil_bias = []
    for window, dil in DILATED_PATTERNS:
        look = window // dil
        tq = min(2 * look, seq // dil)
        dil_bias.append(bias_expand(dilated_bias_buckets(tq, look, dil), rel_bias))

    for i in range(depth):
        j = i // 2
        if i % 2 == 0:
            w_in = a_w_in[j]
            kv_end = Q_LORA + KV_LORA
            w_in_p = jnp.concatenate(
                [w_in[:, :kv_end + IDX_DIM], jnp.zeros((d, LANES - IDX_DIM), F32),
                 w_in[:, kv_end + IDX_DIM:], jnp.zeros((d, LANES - IDX_HEADS), F32)],
                axis=1).astype(BF16)
            qlat, qidx, widx, ckv, kidx = dsa_prologue(
                xb, w_in_p, a_q_norm[j][None], a_kv_norm[j][None], a_w_uq[j].astype(BF16),
                a_w_uq_idx[j].astype(BF16), a_w_uk[j].astype(BF16), tm=256)
            o = dsa_attention(qlat, qidx, widx, ckv, kidx, dsa_tab, a_w_uv[j].astype(BF16),
                              bsz, seq)
            o_parts, lse_parts, w_o = [o], [], a_w_o[j]
        else:
            qkv = matmul(xb, b_w_in[j].astype(BF16), BF16, tm=512, tn=1024,
                         scaled_blocks=hw // 1024, scale=HEAD_DIM ** -0.5)
            o_parts, lse_parts = [], []
            for (window, dil), bias in zip(DILATED_PATTERNS, dil_bias):
                o_i, lse_i = dilated_branch(qkv, bias, bsz, seq, dil, window // dil)
                o_parts.append(o_i)
                lse_parts.append(lse_i)
            w_o = b_w_o[j]
        x, xb = post_attention(o_parts, lse_parts, w_o.astype(BF16), x, ln1_g[i][None],
                               ln1_b[i][None], alpha, tm=256)
        x, xb = moe_layer(x, router_wt, router_b, moe_w_gate[i], moe_w_up[i], moe_w_down[i],
                          ln2_g[i][None], ln2_b[i][None], alpha)
    return x.reshape(bsz, seq, d)
```

```python
import functools
import math

import numpy as np
import jax
import jax.numpy as jnp
from jax import lax
from jax.experimental import pallas as pl
from jax.experimental.pallas import tpu as pltpu

F32 = jnp.float32
BF16 = jnp.bfloat16
I32 = jnp.int32

N_HEADS = 16
HEAD_DIM = 128
Q_LORA = 512
KV_LORA = 256
IDX_HEADS = 16
IDX_DIM = 64
IDX_TOPK_MAX = 256
DILATED_PATTERNS = ((128, 1), (512, 4), (2048, 16))
N_BUCKETS = 32
MAX_DISTANCE = 2048
N_EXPERTS = 16
N_GROUPS = 4
EXPERTS_PER_GROUP = N_EXPERTS // N_GROUPS
D_EXPERT = 512
EPS = 1e-5

LANES = 128
NEG = -1e30
INT_MIN = -(2 ** 31)
MASK_BUCKET = N_BUCKETS
VMEM_LIMIT = 56 * 1024 * 1024


def _cparams(*sem):
    return pltpu.CompilerParams(dimension_semantics=sem, vmem_limit_bytes=VMEM_LIMIT)


def _resident(shape, index_map):
    return pl.BlockSpec(shape, index_map, pipeline_mode=pl.Buffered(1))


def _t5_bucket(dist):
    n = jnp.maximum(dist, 0)
    max_exact = N_BUCKETS // 2
    nf = jnp.maximum(n, max_exact).astype(F32)
    large = max_exact + (jnp.log(nf / max_exact) / math.log(MAX_DISTANCE / max_exact)
                         * (N_BUCKETS - max_exact)).astype(I32)
    large = jnp.minimum(large, N_BUCKETS - 1)
    return jnp.where(n < max_exact, n, large)


def _layer_norm(v, g, b):
    mu = jnp.mean(v, axis=-1, keepdims=True)
    c = v - mu
    var = jnp.mean(c * c, axis=-1, keepdims=True)
    return c * lax.rsqrt(var + EPS) * g + b


def _rms_norm(v, g):
    return v * lax.rsqrt(jnp.mean(v * v, axis=-1, keepdims=True) + EPS) * g


def _bias_expand_body(bucket_ref, rb_ref, o_ref):
    bk = bucket_ref[0]
    for h in range(N_HEADS):
        acc = jnp.where(bk == MASK_BUCKET, NEG, 0.0).astype(F32)
        for b in range(N_BUCKETS):
            acc = jnp.where(bk == b, rb_ref[b, h], acc)
        o_ref[0, h] = acc


def bias_expand(bucket, rel_bias):
    n, r, c = bucket.shape
    return pl.pallas_call(
        _bias_expand_body,
        grid=(n,),
        in_specs=[pl.BlockSpec((1, r, c), lambda i: (i, 0, 0)),
                  pl.BlockSpec(memory_space=pltpu.SMEM)],
        out_specs=pl.BlockSpec((1, N_HEADS, r, c), lambda i: (i, 0, 0, 0)),
        out_shape=jax.ShapeDtypeStruct((n, N_HEADS, r, c), F32),
        compiler_params=_cparams("arbitrary"),
        name="bias_expand",
    )(bucket, rel_bias)


def _qkv_body(x_ref, w_ref, *refs, dils, scaled_blocks, scale):
    out_refs, acc_ref = refs[:len(dils)], refs[len(dils)]
    tm = x_ref.shape[0]
    acc = jnp.dot(x_ref[...], w_ref[...], preferred_element_type=F32)
    acc = acc * jnp.where(pl.program_id(0) < scaled_blocks, scale, 1.0).astype(F32)
    n_slab = acc.shape[1] // LANES
    for s in range(n_slab):
        acc_ref[s] = acc[:, s * LANES:(s + 1) * LANES]
    for o_ref, dil in zip(out_refs, dils):
        if dil == 1:
            o_ref[0, 0] = acc.astype(o_ref.dtype)
            continue
        for c in range(dil):
            cols = [acc_ref[s, pl.ds(c, tm // dil, stride=dil), :] for s in range(n_slab)]
            o_ref[0, c] = jnp.concatenate(cols, axis=1).astype(o_ref.dtype)


def qkv_projection(x, w, bsz, seq, dils, tm, tn, scaled_blocks, scale):
    m, k = x.shape
    n = w.shape[1]
    tpb = seq // tm
    out_spec = lambda dil: pl.BlockSpec((1, dil, tm // dil, tn),
                                        lambda j, i: (i // tpb, 0, i % tpb, j))
    return pl.pallas_call(
        functools.partial(_qkv_body, dils=dils, scaled_blocks=scaled_blocks, scale=scale),
        grid=(n // tn, m // tm),
        in_specs=[pl.BlockSpec((tm, k), lambda j, i: (i, 0)),
                  pl.BlockSpec((k, tn), lambda j, i: (0, j))],
        out_specs=[out_spec(dil) for dil in dils],
        out_shape=[jax.ShapeDtypeStruct((bsz, dil, seq // dil, n), BF16) for dil in dils],
        scratch_shapes=[pltpu.VMEM((tn // LANES, tm, LANES), F32)],
        compiler_params=_cparams("arbitrary", "arbitrary"),
        name="qkv_projection",
    )(x, w)


def _dsa_pro_body(x_ref, win_ref, qn_ref, kvn_ref, wuq_ref, wuqi_ref, wuk_ref,
                  qlat_ref, qidx_ref, widx_ref, ckv_ref, kidx_ref):
    proj = jnp.dot(x_ref[...], win_ref[...], preferred_element_type=F32)
    c_q = _rms_norm(proj[:, :Q_LORA], qn_ref[...]).astype(BF16)
    c_kv = _rms_norm(proj[:, Q_LORA:Q_LORA + KV_LORA], kvn_ref[...])
    ckv_ref[...] = c_kv.astype(BF16)
    kidx_ref[...] = proj[:, 768:768 + IDX_DIM].astype(BF16)
    widx_ref[...] = proj[:, 896:1024] * (IDX_HEADS ** -0.5)
    q = jnp.dot(c_q, wuq_ref[...], preferred_element_type=F32)
    for h in range(N_HEADS):
        qh = q[:, h * HEAD_DIM:(h + 1) * HEAD_DIM].astype(BF16)
        ql = jnp.dot(qh, wuk_ref[h], preferred_element_type=F32) * (HEAD_DIM ** -0.5)
        qlat_ref[h] = ql.astype(BF16)
    qi = jnp.dot(c_q, wuqi_ref[...], preferred_element_type=F32) * (IDX_DIM ** -0.5)
    for h in range(IDX_HEADS):
        qidx_ref[h] = qi[:, h * IDX_DIM:(h + 1) * IDX_DIM].astype(BF16)


def dsa_prologue(xb, w_in_p, q_norm, kv_norm, w_uq, w_uq_idx, w_uk, tm):
    n, d = xb.shape
    hw = N_HEADS * HEAD_DIM
    return pl.pallas_call(
        _dsa_pro_body,
        grid=(n // tm,),
        in_specs=[pl.BlockSpec((tm, d), lambda i: (i, 0)),
                  _resident((d, 1024), lambda i: (0, 0)),
                  _resident((1, Q_LORA), lambda i: (0, 0)),
                  _resident((1, KV_LORA), lambda i: (0, 0)),
                  _resident((Q_LORA, hw), lambda i: (0, 0)),
                  _resident((Q_LORA, IDX_HEADS * IDX_DIM), lambda i: (0, 0)),
                  _resident((N_HEADS, HEAD_DIM, KV_LORA), lambda i: (0, 0, 0))],
        out_specs=[pl.BlockSpec((N_HEADS, tm, KV_LORA), lambda i: (0, i, 0)),
                   pl.BlockSpec((IDX_HEADS, tm, IDX_DIM), lambda i: (0, i, 0)),
                   pl.BlockSpec((tm, LANES), lambda i: (i, 0)),
                   pl.BlockSpec((tm, KV_LORA), lambda i: (i, 0)),
                   pl.BlockSpec((tm, IDX_DIM), lambda i: (i, 0))],
        out_shape=[jax.ShapeDtypeStruct((N_HEADS, n, KV_LORA), BF16),
                   jax.ShapeDtypeStruct((IDX_HEADS, n, IDX_DIM), BF16),
                   jax.ShapeDtypeStruct((n, LANES), F32),
                   jax.ShapeDtypeStruct((n, KV_LORA), BF16),
                   jax.ShapeDtypeStruct((n, IDX_DIM), BF16)],
        compiler_params=_cparams("arbitrary"),
        name="dsa_prologue",
    )(xb, w_in_p, q_norm, kv_norm, w_uq, w_uq_idx, w_uk)


DSA_TQ = 128
DSA_TK = 256


def _dsa_attn_body(qlat_ref, qidx_ref, widx_ref, ckv_ref, kidx_ref, tab_ref, wuv_ref, o_ref,
                   keys_ref, acc_ref, m_ref, wib_ref, pstar_ref,
                   s0_ref, s1_ref, p0_ref, p1_ref, a0_ref, a1_ref,
                   *, seq, topk, n_tab):
    tq, tk = DSA_TQ, DSA_TK
    i = pl.program_id(1)
    q0 = i * tq
    nk = (q0 + tq + tk - 1) // tk
    qpos = q0 + lax.broadcasted_iota(I32, (tq, tk), 0)
    col = lax.broadcasted_iota(I32, (tq, tk), 1)

    def dup(v):
        return jnp.concatenate([v] * (tk // LANES), axis=1)

    for h in range(IDX_HEADS):
        wib_ref[h] = jnp.broadcast_to(widx_ref[:, h:h + 1], (tq, LANES))

    def score_tile(j, carry):
        k0 = pl.multiple_of(j * tk, tk)
        kt = kidx_ref[pl.ds(k0, tk), :]
        s = jnp.zeros((tq, tk), F32)
        for h in range(IDX_HEADS):
            rel = lax.dot_general(qidx_ref[h], kt, (((1,), (1,)), ((), ())),
                                  preferred_element_type=F32)
            s = s + jnp.maximum(rel, 0.0) * dup(wib_ref[h])
        bits = pltpu.bitcast(s, I32)
        key = bits ^ ((bits >> 31) & 0x7FFFFFFF)
        keys_ref[j] = jnp.where(k0 + col <= qpos, key, INT_MIN)
        return carry

    lax.fori_loop(0, nk, score_tile, 0)

    def count(pred):
        def body(j, acc):
            ind = jnp.where(pred(keys_ref[j], j), 1.0, 0.0)
            part = ind[:, :LANES]
            for c in range(1, tk // LANES):
                part = part + ind[:, c * LANES:(c + 1) * LANES]
            return acc + part
        acc = lax.fori_loop(0, nk, body, jnp.zeros((tq, LANES), F32))
        return jnp.sum(acc, axis=1, keepdims=True)

    def bit_step(t, carry):
        v, cge = carry
        cand_u = v | jnp.left_shift(jnp.int32(1), 31 - t)
        cand = dup(cand_u ^ INT_MIN)
        cnt = count(lambda key, j: key >= cand)
        take = cnt >= topk
        return jnp.where(take, cand_u, v), jnp.where(take, cnt, cge)

    v, cge = lax.fori_loop(0, 32, bit_step,
                           (jnp.zeros((tq, LANES), I32), jnp.zeros((tq, 1), F32)))
    thr = v ^ INT_MIN
    thr_w = dup(thr)

    pstar_ref[...] = jnp.full((tq, LANES), seq, I32)
    surplus = jnp.where((v[:, :1] != 0) & (cge > topk), 1.0, 0.0)

    @pl.when(jnp.max(surplus) > 0.0)
    def _():
        cgt = count(lambda key, j: key > thr_w)
        need = topk - cgt

        def pos_step(t, p):
            cand = p | jnp.left_shift(jnp.int32(1), (seq.bit_length() - 1) - t)
            cw = dup(cand)
            before = count(lambda key, j: (key == thr_w) & (j * tk + col < cw))
            return jnp.where(before < need, cand, p)

        p = lax.fori_loop(0, seq.bit_length(), pos_step, jnp.zeros((tq, LANES), I32))
        pstar_ref[...] = jnp.where(surplus > 0.0, p, seq)

    pstar_w = dup(pstar_ref[...])

    acc_ref[...] = jnp.zeros_like(acc_ref)
    m_ref[...] = jnp.full_like(m_ref, NEG)
    dq = q0 // LANES

    n_tiles = seq // tk
    qlat2d = lambda: qlat_ref[...].reshape(N_HEADS * tq, KV_LORA)

    def kv_tile(j):
        return ckv_ref[pl.ds(pl.multiple_of(jnp.clip(j, 0, n_tiles - 1) * tk, tk), tk), :]

    def logits(j):
        return lax.dot_general(qlat2d(), kv_tile(j), (((1,), (1,)), ((), ())),
                               preferred_element_type=F32)

    s_bufs, p_bufs, alpha_bufs = (s0_ref, s1_ref), (p0_ref, p1_ref), (a0_ref, a1_ref)
    s0_ref[...] = logits(0)
    p1_ref[...] = jnp.zeros(p1_ref.shape, BF16)
    a1_ref[...] = jnp.ones(a1_ref.shape, F32)

    def attn_step(j, cur):
        prv = 1 - cur
        s_ref, p_ref, alpha_ref = s_bufs[cur], p_bufs[cur], alpha_bufs[cur]
        s_bufs[prv][...] = logits(j + 1)

        kpos = j * tk + col
        key = keys_ref[jnp.minimum(j, nk - 1)]
        sel = (kpos <= qpos) & ((key > thr_w) | ((key == thr_w) & (kpos <= pstar_w)))
        maskadd = jnp.where(sel, 0.0, NEG)
        dbase = dq - j * (tk // LANES) + 1
        didx = [jnp.clip(dbase - c, 0, n_tab - 1) for c in range(tk // LANES)]
        for h in range(N_HEADS):
            rows = slice(h * tq, (h + 1) * tq)
            bias = jnp.concatenate([tab_ref[d, h] for d in didx], axis=1)
            s = s_ref[rows, :] + bias + maskadd
            s_ref[rows, :] = s
            m_old = m_ref[rows, :]
            m_new = jnp.maximum(m_old, jnp.max(s, axis=1, keepdims=True))
            alpha_ref[rows, :] = jnp.exp(m_old - m_new)
            m_ref[rows, :] = m_new

        kv_ext = jnp.concatenate([kv_tile(j - 1), jnp.ones((tk, LANES), BF16)], axis=1)
        pv = jnp.dot(p_bufs[prv][...], kv_ext, preferred_element_type=F32)
        acc_ref[...] = alpha_bufs[prv][...] * acc_ref[...] + pv

        for h in range(N_HEADS):
            rows = slice(h * tq, (h + 1) * tq)
            p_ref[rows, :] = jnp.exp(s_ref[rows, :] - m_ref[rows, :]).astype(BF16)

    def attn_pair(jj, carry):
        attn_step(2 * jj, 0)
        attn_step(2 * jj + 1, 1)
        return carry

    lax.fori_loop(0, nk // 2 + 1, attn_pair, 0)

    outs = []
    for h in range(N_HEADS):
        rows = slice(h * tq, (h + 1) * tq)
        inv_l = 1.0 / acc_ref[rows, KV_LORA:]
        o_lat = acc_ref[rows, :KV_LORA] * jnp.concatenate([inv_l] * (KV_LORA // LANES), axis=1)
        outs.append(jnp.dot(o_lat.astype(BF16), wuv_ref[h], preferred_element_type=F32))
    o_ref[...] = jnp.concatenate(outs, axis=1).astype(o_ref.dtype)


def dsa_attention(qlat, qidx, widx, ckv, kidx, tab, w_uv, bsz, seq):
    tq, tk = DSA_TQ, DSA_TK
    nq = seq // tq
    n = bsz * seq
    n_tab = tab.shape[0]
    topk = min(IDX_TOPK_MAX, seq // 4)
    body = functools.partial(_dsa_attn_body, seq=seq, topk=topk, n_tab=n_tab)
    return pl.pallas_call(
        body,
        grid=(bsz, nq),
        in_specs=[pl.BlockSpec((N_HEADS, tq, KV_LORA), lambda b, i: (0, b * nq + i, 0)),
                  pl.BlockSpec((IDX_HEADS, tq, IDX_DIM), lambda b, i: (0, b * nq + i, 0)),
                  pl.BlockSpec((tq, LANES), lambda b, i: (b * nq + i, 0)),
                  pl.BlockSpec((seq, KV_LORA), lambda b, i: (b, 0)),
                  pl.BlockSpec((seq, IDX_DIM), lambda b, i: (b, 0)),
                  _resident(tab.shape, lambda b, i: (0, 0, 0, 0)),
                  _resident((N_HEADS, KV_LORA, HEAD_DIM), lambda b, i: (0, 0, 0))],
        out_specs=pl.BlockSpec((tq, N_HEADS * HEAD_DIM), lambda b, i: (b * nq + i, 0)),
        out_shape=jax.ShapeDtypeStruct((n, N_HEADS * HEAD_DIM), BF16),
        scratch_shapes=[pltpu.VMEM((seq // tk, tq, tk), I32),
                        pltpu.VMEM((N_HEADS * tq, KV_LORA + LANES), F32),
                        pltpu.VMEM((N_HEADS * tq, 1), F32),
                        pltpu.VMEM((IDX_HEADS, tq, LANES), F32),
                        pltpu.VMEM((tq, LANES), I32),
                        pltpu.VMEM((N_HEADS * tq, tk), F32),
                        pltpu.VMEM((N_HEADS * tq, tk), F32),
                        pltpu.VMEM((N_HEADS * tq, tk), BF16),
                        pltpu.VMEM((N_HEADS * tq, tk), BF16),
                        pltpu.VMEM((N_HEADS * tq, 1), F32),
                        pltpu.VMEM((N_HEADS * tq, 1), F32)],
        compiler_params=_cparams("arbitrary", "arbitrary"),
        name="dsa_attention",
    )(qlat, qidx, widx, ckv, kidx, tab, w_uv)


def dsa_bias_buckets(seq):
    far = np.maximum(np.arange(seq), N_BUCKETS // 2).astype(np.float64)
    far_bucket = np.floor(np.log(far / (N_BUCKETS // 2)) / math.log(MAX_DISTANCE / (N_BUCKETS // 2))
                          * (N_BUCKETS - N_BUCKETS // 2))
    unsat = np.nonzero(far_bucket < N_BUCKETS - 1 - N_BUCKETS // 2)[0]
    n_sat = int(unsat[-1]) + 1 if unsat.size else 0
    n_tab = min(seq // LANES + 1, -(-(n_sat + LANES - 1) // LANES) + 2)
    t = np.arange(n_tab)[:, None, None]
    qi = np.arange(LANES)[None, :, None]
    ki = np.arange(LANES)[None, None, :]
    dist = np.maximum(LANES * (t - 1) + qi - ki, 0)
    return _t5_bucket(jnp.asarray(dist, I32))


def _dil_body(q_ref, kp_ref, km_ref, vp_ref, vm_ref, bias_ref, o_ref, lse_ref, *, look):
    nblk = pl.program_id(2)
    tq = q_ref.shape[2]
    ncol = look + tq
    colv = lax.broadcasted_iota(I32, (1, ncol), 1)
    first = jnp.where((nblk == 0) & (colv < look), NEG, 0.0)
    lse_ref[0, 0] = jnp.zeros(lse_ref.shape[2:], F32)
    outs = []
    for h in range(N_HEADS):
        sl = slice(h * HEAD_DIM, (h + 1) * HEAD_DIM)
        q = q_ref[0, 0, :, sl]
        k = jnp.concatenate([kp_ref[0, 0, :, sl], km_ref[0, 0, :, sl]], axis=0)
        v = jnp.concatenate([vp_ref[0, 0, :, sl], vm_ref[0, 0, :, sl]], axis=0)
        s = lax.dot_general(q, k, (((1,), (1,)), ((), ())), preferred_element_type=F32)
        s = s + bias_ref[0, h] + first
        m = jnp.max(s, axis=1, keepdims=True)
        p = jnp.exp(s - m)
        l = jnp.sum(p, axis=1, keepdims=True)
        o = jnp.dot(p.astype(BF16), v, preferred_element_type=F32) * (1.0 / l)
        outs.append(o)
        lse_ref[0, 0, :, h:h + 1] = m + jnp.log(l)
    o_ref[0, 0] = jnp.concatenate(outs, axis=1).astype(o_ref.dtype)


def dilated_branch(qkv, bias, look):
    bsz, dil, ls, _ = qkv.shape
    hw = N_HEADS * HEAD_DIM
    tq = bias.shape[2]
    nb = ls // tq
    r = tq // look
    prev = lambda part: (lambda b, c, n: (b, c, jnp.maximum(n * r - 1, 0), part))
    main = lambda part: (lambda b, c, n: (b, c, n, part))
    return pl.pallas_call(
        functools.partial(_dil_body, look=look),
        grid=(bsz, dil, nb),
        in_specs=[pl.BlockSpec((1, 1, tq, hw), main(0)),
                  pl.BlockSpec((1, 1, look, hw), prev(1)),
                  pl.BlockSpec((1, 1, tq, hw), main(1)),
                  pl.BlockSpec((1, 1, look, hw), prev(2)),
                  pl.BlockSpec((1, 1, tq, hw), main(2)),
                  _resident(bias.shape, lambda b, c, n: (0, 0, 0, 0))],
        out_specs=[pl.BlockSpec((1, 1, tq, hw), lambda b, c, n: (b, c, n, 0)),
                   pl.BlockSpec((1, 1, tq, LANES), lambda b, c, n: (b, c, n, 0))],
        out_shape=[jax.ShapeDtypeStruct((bsz, dil, ls, hw), BF16),
                   jax.ShapeDtypeStruct((bsz, dil, ls, LANES), F32)],
        compiler_params=_cparams("arbitrary", "arbitrary", "arbitrary"),
        name=f"dilated_d{dil}",
    )(qkv, qkv, qkv, qkv, qkv, bias)


def dilated_bias_buckets(tq, look, dil):
    rel = np.arange(tq)[:, None] + look - np.arange(look + tq)[None, :]
    band = (rel >= 0) & (rel <= look)
    bucket = _t5_bucket(jnp.asarray(dil * np.maximum(rel, 0), I32))
    return jnp.where(jnp.asarray(band), bucket, MASK_BUCKET)[None]


def _post_attn_body(*refs, dils, alpha):
    n_parts = len(dils)
    merge = n_parts > 1
    o_refs = refs[:n_parts]
    lse_refs = refs[n_parts:2 * n_parts] if merge else ()
    base = n_parts + len(lse_refs)
    wo_ref, x_ref, g_ref, b_ref, xo_ref, xb_ref = refs[base:base + 6]
    scratch = list(refs[base + 6:])
    tm = x_ref.shape[0]
    head = lambda h: slice(h * HEAD_DIM, (h + 1) * HEAD_DIM)
    if not merge:
        o = o_refs[0][0, 0]
    else:
        lses, o_heads = [], []
        for part, dil in enumerate(dils):
            if dil == 1:
                lses.append(lse_refs[part][0, 0])
                o_heads.append(lambda h, r=o_refs[part]: r[0, 0, :, head(h)].astype(F32))
                continue
            o_scr, l_scr = scratch.pop(0), scratch.pop(0)
            rows = tm // dil
            for c in range(dil):
                l_scr[pl.ds(c, rows, stride=dil), :] = lse_refs[part][0, c]
                for h in range(N_HEADS):
                    o_scr[h, pl.ds(c, rows, stride=dil), :] = (
                        o_refs[part][0, c, :, head(h)].astype(F32))
            lses.append(l_scr[...])
            o_heads.append(lambda h, r=o_scr: r[h])
        mx = functools.reduce(jnp.maximum, lses)
        ws = [jnp.exp(v - mx) for v in lses]
        inv = 1.0 / functools.reduce(lambda a, b: a + b, ws)
        ws = [w * inv for w in ws]
        cols = []
        for h in range(N_HEADS):
            acc = None
            for part in range(n_parts):
                term = ws[part][:, h:h + 1] * o_heads[part](h)
                acc = term if acc is None else acc + term
            cols.append(acc)
        o = jnp.concatenate(cols, axis=1).astype(BF16)
    hproj = jnp.dot(o, wo_ref[...], preferred_element_type=F32)
    y = _layer_norm(alpha * x_ref[...] + hproj, g_ref[...], b_ref[...])
    xo_ref[...] = y
    xb_ref[...] = y.astype(BF16)


def post_attention(o_parts, lse_parts, w_o, x, g, b, alpha, tm):
    n, d = x.shape
    hw = N_HEADS * HEAD_DIM
    dils = tuple(o.shape[1] for o in o_parts)
    seq = o_parts[0].shape[1] * o_parts[0].shape[2]
    tpb = seq // tm
    row = lambda width: pl.BlockSpec((tm, width), lambda i: (i, 0))
    by_class = lambda dil, width: pl.BlockSpec((1, dil, tm // dil, width),
                                               lambda i: (i // tpb, 0, i % tpb, 0))
    in_specs = ([by_class(dil, hw) for dil in dils]
                + [by_class(dil, LANES) for dil in dils[:len(lse_parts)]]
                + [_resident((hw, d), lambda i: (0, 0)), row(d),
                   _resident((1, d), lambda i: (0, 0)), _resident((1, d), lambda i: (0, 0))])
    scratch = []
    for dil in dils:
        if dil > 1:
            scratch += [pltpu.VMEM((N_HEADS, tm, HEAD_DIM), F32), pltpu.VMEM((tm, LANES), F32)]
    return pl.pallas_call(
        functools.partial(_post_attn_body, dils=dils, alpha=alpha),
        grid=(n // tm,),
        in_specs=in_specs,
        out_specs=[row(d), row(d)],
        out_shape=[jax.ShapeDtypeStruct((n, d), F32), jax.ShapeDtypeStruct((n, d), BF16)],
        scratch_shapes=scratch,
        compiler_params=_cparams("arbitrary"),
        name=f"post_attention_{len(dils)}",
    )(*o_parts, *lse_parts, w_o, x, g, b)


def _router_body(x_ref, rwt_ref, rb_ref, route_ref, gates_ref, counts_ref, carry_ref):
    tm = x_ref.shape[0]

    @pl.when(pl.program_id(0) == 0)
    def _():
        carry_ref[...] = jnp.zeros_like(carry_ref)

    logits = lax.dot_general(rwt_ref[...], x_ref[...], (((1,), (1,)), ((), ())),
                             preferred_element_type=F32,
                             precision=lax.Precision.HIGHEST)
    rows = [logits[e:e + 1, :] for e in range(N_EXPERTS)]
    mx = functools.reduce(jnp.maximum, rows)
    ex = [jnp.exp(r - mx) for r in rows]
    inv = 1.0 / functools.reduce(lambda a, b: a + b, ex)
    probs = [e * inv for e in ex]
    sel = [probs[e] + rb_ref[e] for e in range(N_EXPERTS)]

    zero_i = jnp.zeros((1, tm), I32)
    best_t = None
    for g in range(N_GROUPS):
        vals = sel[g * EXPERTS_PER_GROUP:(g + 1) * EXPERTS_PER_GROUP]
        prb = probs[g * EXPERTS_PER_GROUP:(g + 1) * EXPERTS_PER_GROUP]
        top, top_i, top_p = vals[0], zero_i, prb[0]
        for j in range(1, EXPERTS_PER_GROUP):
            c = vals[j] > top
            top = jnp.where(c, vals[j], top)
            top_i = jnp.where(c, j, top_i)
            top_p = jnp.where(c, prb[j], top_p)
        sec = jnp.full((1, tm), -jnp.inf, F32)
        sec_i, sec_p = zero_i, jnp.zeros((1, tm), F32)
        for j in range(EXPERTS_PER_GROUP):
            c = (vals[j] > sec) & (top_i != j)
            sec = jnp.where(c, vals[j], sec)
            sec_i = jnp.where(c, j, sec_i)
            sec_p = jnp.where(c, prb[j], sec_p)
        tot = top + sec
        if best_t is None:
            best_t = tot
            e0, e1 = top_i + g * EXPERTS_PER_GROUP, sec_i + g * EXPERTS_PER_GROUP
            p0, p1 = top_p, sec_p
        else:
            c = tot > best_t
            best_t = jnp.where(c, tot, best_t)
            e0 = jnp.where(c, top_i + g * EXPERTS_PER_GROUP, e0)
            e1 = jnp.where(c, sec_i + g * EXPERTS_PER_GROUP, e1)
            p0 = jnp.where(c, top_p, p0)
            p1 = jnp.where(c, sec_p, p1)
    ginv = 1.0 / (p0 + p1)
    g0, g1 = p0 * ginv, p1 * ginv
    gates_ref[...] = jnp.concatenate([g0, g1, jnp.zeros((LANES - 2, tm), F32)], axis=0).T

    assign = jnp.concatenate([jnp.where((e0 == e) | (e1 == e), 1.0, 0.0)
                              for e in range(N_EXPERTS)], axis=0).astype(BF16)
    upper = jnp.where(lax.broadcasted_iota(I32, (tm, tm), 0)
                      <= lax.broadcasted_iota(I32, (tm, tm), 1), 1.0, 0.0).astype(BF16)
    prefix = jnp.dot(assign, upper, preferred_element_type=F32) + carry_ref[...]
    rank0 = jnp.zeros((1, tm), F32)
    rank1 = jnp.zeros((1, tm), F32)
    for e in range(N_EXPERTS):
        rank0 = jnp.where(e0 == e, prefix[e:e + 1, :], rank0)
        rank1 = jnp.where(e1 == e, prefix[e:e + 1, :], rank1)
    total = prefix[:, tm - 1:tm]
    carry_ref[...] = total
    counts_ref[...] = jnp.broadcast_to(total, counts_ref.shape).astype(I32)
    route_ref[...] = jnp.concatenate(
        [e0, e1, (rank0 - 1.0).astype(I32), (rank1 - 1.0).astype(I32), jnp.zeros((4, tm), I32)],
        axis=0)


def router(x, router_wt, router_b, tm):
    n, d = x.shape
    return pl.pallas_call(
        _router_body,
        grid=(n // tm,),
        in_specs=[pl.BlockSpec((tm, d), lambda i: (i, 0)),
                  _resident((N_EXPERTS, d), lambda i: (0, 0)),
                  pl.BlockSpec(memory_space=pltpu.SMEM)],
        out_specs=[pl.BlockSpec((8, tm), lambda i: (0, i)),
                   pl.BlockSpec((tm, LANES), lambda i: (i, 0)),
                   pl.BlockSpec((N_EXPERTS, LANES), lambda i: (0, 0))],
        out_shape=[jax.ShapeDtypeStruct((8, n), I32),
                   jax.ShapeDtypeStruct((n, LANES), F32),
                   jax.ShapeDtypeStruct((N_EXPERTS, LANES), I32)],
        scratch_shapes=[pltpu.VMEM((N_EXPERTS, 1), F32)],
        compiler_params=_cparams("arbitrary"),
        name="router",
    )(x, router_wt, router_b)


EXPERT_TILE = 512
COMBINE_TILE = 256


def _row_copy(src_hbm, row, dst, dst_row, sem):
    return pltpu.make_async_copy(src_hbm.at[pl.ds(row, 1), :], dst.at[pl.ds(dst_row, 1), :], sem)


def _experts_body(te_ref, nu_ref, src_next_ref, src_first_ref, x_hbm, wg_ref, wu_ref, wd_ref,
                  ys_ref, xbuf, sem, wgb, wub, wdb):
    i = pl.program_id(0)
    n_used = nu_ref[0]
    slot = i % 2
    tmx = ys_ref.shape[0]

    def issue(src_ref, dst_slot):
        def body(r, c):
            _row_copy(x_hbm, src_ref[0, 0, r], xbuf.at[dst_slot], r, sem.at[dst_slot]).start()
            return c
        lax.fori_loop(0, tmx, body, 0, unroll=8)

    @pl.when(i == 0)
    def _():
        issue(src_first_ref, 0)

    @pl.when(i + 1 < n_used)
    def _():
        issue(src_next_ref, 1 - slot)

    @pl.when(i < n_used)
    def _():
        def wait_row(r, c):
            _row_copy(x_hbm, 0, xbuf.at[slot], r, sem.at[slot]).wait()
            return c
        lax.fori_loop(0, tmx, wait_row, 0, unroll=8)

        @pl.when((i == 0) | (te_ref[i] != te_ref[jnp.maximum(i - 1, 0)]))
        def _():
            wgb[...] = wg_ref[0, 0].astype(BF16)
            wub[...] = wu_ref[0, 0].astype(BF16)
            wdb[...] = wd_ref[0, 0].astype(BF16)

        xb = xbuf[slot].astype(BF16)
        gate = jnp.dot(xb, wgb[...], preferred_element_type=F32)
        up = jnp.dot(xb, wub[...], preferred_element_type=F32)
        hidden = (gate * jax.nn.sigmoid(gate)) * up
        ys_ref[...] = jnp.dot(hidden.astype(BF16), wdb[...], preferred_element_type=F32)

    @pl.when(i >= n_used)
    def _():
        ys_ref[...] = jnp.zeros_like(ys_ref)


def moe_experts(x, src_tiles, tile_expert, n_used, w_gate, w_up, w_down, layer):
    n, d = x.shape
    n_tiles = src_tiles.shape[0]
    tmx = EXPERT_TILE
    smem_tile = lambda index_map: pl.BlockSpec((1, 1, tmx), index_map, memory_space=pltpu.SMEM)
    by_expert = lambda shape: pl.BlockSpec((1,) + shape, lambda i, te, nu: (layer, te[i], 0, 0))
    grid_spec = pltpu.PrefetchScalarGridSpec(
        num_scalar_prefetch=2,
        grid=(n_tiles,),
        in_specs=[smem_tile(lambda i, te, nu: (jnp.minimum(i + 1, n_tiles - 1), 0, 0)),
                  smem_tile(lambda i, te, nu: (0, 0, 0)),
                  pl.BlockSpec(memory_space=pl.ANY),
                  by_expert((1, d, D_EXPERT)), by_expert((1, d, D_EXPERT)),
                  by_expert((1, D_EXPERT, d))],
        out_specs=pl.BlockSpec((tmx, d), lambda i, te, nu: (i, 0)),
        scratch_shapes=[pltpu.VMEM((2, tmx, d), F32),
                        pltpu.SemaphoreType.DMA((2,)),
                        pltpu.VMEM((d, D_EXPERT), BF16),
                        pltpu.VMEM((d, D_EXPERT), BF16),
                        pltpu.VMEM((D_EXPERT, d), BF16)])
    return pl.pallas_call(
        _experts_body,
        grid_spec=grid_spec,
        out_shape=jax.ShapeDtypeStruct((n_tiles * tmx, d), F32),
        compiler_params=_cparams("arbitrary"),
        name="moe_experts",
    )(tile_expert, n_used, src_tiles, src_tiles, x, w_gate, w_up, w_down)


def _combine_body(pos_next_ref, pos_first_ref, ys_hbm, gates_ref, x_ref, g_ref, b_ref,
                  xo_ref, xbo_ref, ybuf, sem, *, alpha):
    i = pl.program_id(0)
    n_steps = pl.num_programs(0)
    slot = i % 2
    tm = x_ref.shape[0]

    def issue(pos_ref, dst_slot):
        for k in range(2):
            def body(r, c):
                _row_copy(ys_hbm, pos_ref[0, k, r], ybuf.at[dst_slot, k], r,
                          sem.at[dst_slot]).start()
                return c
            lax.fori_loop(0, tm, body, 0, unroll=8)

    @pl.when(i == 0)
    def _():
        issue(pos_first_ref, 0)

    @pl.when(i + 1 < n_steps)
    def _():
        issue(pos_next_ref, 1 - slot)

    def wait_row(r, c):
        for k in range(2):
            _row_copy(ys_hbm, 0, ybuf.at[slot, k], r, sem.at[slot]).wait()
        return c
    lax.fori_loop(0, tm, wait_row, 0, unroll=8)

    gates = gates_ref[...]
    y = gates[:, 0:1] * ybuf[slot, 0] + gates[:, 1:2] * ybuf[slot, 1]
    out = _layer_norm(alpha * x_ref[...] + y, g_ref[...], b_ref[...])
    xo_ref[...] = out
    xbo_ref[...] = out.astype(BF16)


def moe_combine(ys, pos_tiles, gates, x, g, b, alpha):
    n, d = x.shape
    tm = COMBINE_TILE
    n_steps = n // tm
    smem_tile = lambda index_map: pl.BlockSpec((1, 2, tm), index_map, memory_space=pltpu.SMEM)
    row = lambda width: pl.BlockSpec((tm, width), lambda i: (i, 0))
    return pl.pallas_call(
        functools.partial(_combine_body, alpha=alpha),
        grid=(n_steps,),
        in_specs=[smem_tile(lambda i: (jnp.minimum(i + 1, n_steps - 1), 0, 0)),
                  smem_tile(lambda i: (0, 0, 0)),
                  pl.BlockSpec(memory_space=pl.ANY),
                  row(LANES), row(d),
                  _resident((1, d), lambda i: (0, 0)), _resident((1, d), lambda i: (0, 0))],
        out_specs=[row(d), row(d)],
        out_shape=[jax.ShapeDtypeStruct((n, d), F32), jax.ShapeDtypeStruct((n, d), BF16)],
        scratch_shapes=[pltpu.VMEM((2, 2, tm, d), F32), pltpu.SemaphoreType.DMA((2,))],
        compiler_params=_cparams("arbitrary"),
        name="moe_combine",
    )(pos_tiles, pos_tiles, ys, gates, x, g, b)


def moe_layer(x, router_wt, router_b, w_gate, w_up, w_down, layer, g, b, alpha):
    n, d = x.shape
    tmx = EXPERT_TILE
    n_tiles = (2 * n) // tmx + N_EXPERTS
    route, gates, counts = router(x, router_wt, router_b, tm=512)
    counts = counts[:, 0]
    tiles_e = (counts + tmx - 1) // tmx
    tile_end = jnp.cumsum(tiles_e)
    row_off = (tile_end - tiles_e) * tmx
    pos0 = row_off[route[0]] + route[2]
    pos1 = row_off[route[1]] + route[3]
    tok = jnp.arange(n, dtype=I32)
    src = jnp.zeros((n_tiles * tmx,), I32).at[pos0].set(tok).at[pos1].set(tok)
    tile_expert = jnp.minimum(
        jnp.searchsorted(tile_end, jnp.arange(n_tiles, dtype=I32), side="right"),
        N_EXPERTS - 1).astype(I32)
    n_used = tile_end[-1:].astype(I32)
    ys = moe_experts(x, src.reshape(n_tiles, 1, tmx), tile_expert, n_used, w_gate, w_up, w_down,
                     layer)
    tm = COMBINE_TILE
    pos_tiles = jnp.stack([pos0, pos1]).reshape(2, n // tm, tm).transpose(1, 0, 2)
    return moe_combine(ys, pos_tiles, gates, x, g, b, alpha)


def kernel(x, rel_bias, router_w, router_b, a_w_in, a_q_norm, a_kv_norm, a_w_uq, a_w_uq_idx,
           a_w_uk, a_w_uv, a_w_o, b_w_in, b_w_o, ln1_g, ln1_b, ln2_g, ln2_b,
           moe_w_gate, moe_w_up, moe_w_down):
    bsz, seq, d = x.shape
    depth = ln1_g.shape[0]
    n = bsz * seq
    hw = N_HEADS * HEAD_DIM
    alpha = (2 * depth) ** 0.25

    x = x.reshape(n, d)
    xb = x.astype(BF16)
    router_wt = router_w.T

    dsa_tab = bias_expand(dsa_bias_buckets(seq), rel_bias)
    dil_bias = []
    for window, dil in DILATED_PATTERNS:
        look = window // dil
        tq = min(2 * look, seq // dil)
        dil_bias.append(bias_expand(dilated_bias_buckets(tq, look, dil), rel_bias))

    for i in range(depth):
        j = i // 2
        if i % 2 == 0:
            w_in = a_w_in[j]
            kv_end = Q_LORA + KV_LORA
            w_in_p = jnp.concatenate(
                [w_in[:, :kv_end + IDX_DIM], jnp.zeros((d, LANES - IDX_DIM), F32),
                 w_in[:, kv_end + IDX_DIM:], jnp.zeros((d, LANES - IDX_HEADS), F32)],
                axis=1).astype(BF16)
            qlat, qidx, widx, ckv, kidx = dsa_prologue(
                xb, w_in_p, a_q_norm[j][None], a_kv_norm[j][None], a_w_uq[j].astype(BF16),
                a_w_uq_idx[j].astype(BF16), a_w_uk[j].astype(BF16), tm=256)
            o = dsa_attention(qlat, qidx, widx, ckv, kidx, dsa_tab, a_w_uv[j].astype(BF16),
                              bsz, seq)
            o_parts, lse_parts, w_o = [o.reshape(bsz, 1, seq, hw)], [], a_w_o[j]
        else:
            dils = tuple(dil for _, dil in DILATED_PATTERNS)
            qkvs = qkv_projection(xb, b_w_in[j].astype(BF16), bsz, seq, dils, tm=512, tn=1024,
                                  scaled_blocks=hw // 1024, scale=HEAD_DIM ** -0.5)
            o_parts, lse_parts = [], []
            for (window, dil), qkv, bias in zip(DILATED_PATTERNS, qkvs, dil_bias):
                o_i, lse_i = dilated_branch(qkv, bias, window // dil)
                o_parts.append(o_i)
                lse_parts.append(lse_i)
            w_o = b_w_o[j]
        x, xb = post_attention(o_parts, lse_parts, w_o.astype(BF16), x, ln1_g[i][None],
                               ln1_b[i][None], alpha, tm=256)
        x, xb = moe_layer(x, router_wt, router_b, moe_w_gate, moe_w_up, moe_w_down, i,
                          ln2_g[i][None], ln2_b[i][None], alpha)
    return x.reshape(bsz, seq, d)
```

```python
import functools
import math

import numpy as np
import jax
import jax.numpy as jnp
from jax import lax
from jax.experimental import pallas as pl
from jax.experimental.pallas import tpu as pltpu

F32 = jnp.float32
BF16 = jnp.bfloat16
I32 = jnp.int32

N_HEADS = 16
HEAD_DIM = 128
Q_LORA = 512
KV_LORA = 256
IDX_HEADS = 16
IDX_DIM = 64
IDX_TOPK_MAX = 256
DILATED_PATTERNS = ((128, 1), (512, 4), (2048, 16))
N_BUCKETS = 32
MAX_DISTANCE = 2048
N_EXPERTS = 16
N_GROUPS = 4
EXPERTS_PER_GROUP = N_EXPERTS // N_GROUPS
D_EXPERT = 512
EPS = 1e-5

LANES = 128
NEG = -1e30
INT_MIN = -(2 ** 31)
MASK_BUCKET = N_BUCKETS
VMEM_LIMIT = 56 * 1024 * 1024


def _cparams(*sem):
    return pltpu.CompilerParams(dimension_semantics=sem, vmem_limit_bytes=VMEM_LIMIT)


def _resident(shape, index_map):
    return pl.BlockSpec(shape, index_map, pipeline_mode=pl.Buffered(1))


def _t5_bucket(dist):
    n = jnp.maximum(dist, 0)
    max_exact = N_BUCKETS // 2
    nf = jnp.maximum(n, max_exact).astype(F32)
    large = max_exact + (jnp.log(nf / max_exact) / math.log(MAX_DISTANCE / max_exact)
                         * (N_BUCKETS - max_exact)).astype(I32)
    large = jnp.minimum(large, N_BUCKETS - 1)
    return jnp.where(n < max_exact, n, large)


def _layer_norm(v, g, b):
    mu = jnp.mean(v, axis=-1, keepdims=True)
    c = v - mu
    var = jnp.mean(c * c, axis=-1, keepdims=True)
    return c * lax.rsqrt(var + EPS) * g + b


def _rms_norm(v, g):
    return v * lax.rsqrt(jnp.mean(v * v, axis=-1, keepdims=True) + EPS) * g


def _bias_expand_body(bucket_ref, rb_ref, o_ref):
    bk = bucket_ref[0]
    for h in range(N_HEADS):
        acc = jnp.where(bk == MASK_BUCKET, NEG, 0.0).astype(F32)
        for b in range(N_BUCKETS):
            acc = jnp.where(bk == b, rb_ref[b, h], acc)
        o_ref[0, h] = acc


def bias_expand(bucket, rel_bias):
    n, r, c = bucket.shape
    return pl.pallas_call(
        _bias_expand_body,
        grid=(n,),
        in_specs=[pl.BlockSpec((1, r, c), lambda i: (i, 0, 0)),
                  pl.BlockSpec(memory_space=pltpu.SMEM)],
        out_specs=pl.BlockSpec((1, N_HEADS, r, c), lambda i: (i, 0, 0, 0)),
        out_shape=jax.ShapeDtypeStruct((n, N_HEADS, r, c), F32),
        compiler_params=_cparams("arbitrary"),
        name="bias_expand",
    )(bucket, rel_bias)


def _qkv_body(x_ref, w_ref, *refs, dils, scaled_blocks, scale):
    out_refs, acc_ref = refs[:len(dils)], refs[len(dils)]
    tm = x_ref.shape[0]
    acc = jnp.dot(x_ref[...], w_ref[...], preferred_element_type=F32)
    acc = acc * jnp.where(pl.program_id(0) < scaled_blocks, scale, 1.0).astype(F32)
    n_slab = acc.shape[1] // LANES
    for s in range(n_slab):
        acc_ref[s] = acc[:, s * LANES:(s + 1) * LANES]
    for o_ref, dil in zip(out_refs, dils):
        if dil == 1:
            o_ref[0, 0] = acc.astype(o_ref.dtype)
            continue
        for c in range(dil):
            cols = [acc_ref[s, pl.ds(c, tm // dil, stride=dil), :] for s in range(n_slab)]
            o_ref[0, c] = jnp.concatenate(cols, axis=1).astype(o_ref.dtype)


def qkv_projection(x, w, bsz, seq, dils, tm, tn, scaled_blocks, scale):
    m, k = x.shape
    n = w.shape[1]
    tpb = seq // tm
    out_spec = lambda dil: pl.BlockSpec((1, dil, tm // dil, tn),
                                        lambda j, i: (i // tpb, 0, i % tpb, j))
    return pl.pallas_call(
        functools.partial(_qkv_body, dils=dils, scaled_blocks=scaled_blocks, scale=scale),
        grid=(n // tn, m // tm),
        in_specs=[pl.BlockSpec((tm, k), lambda j, i: (i, 0)),
                  pl.BlockSpec((k, tn), lambda j, i: (0, j))],
        out_specs=[out_spec(dil) for dil in dils],
        out_shape=[jax.ShapeDtypeStruct((bsz, dil, seq // dil, n), BF16) for dil in dils],
        scratch_shapes=[pltpu.VMEM((tn // LANES, tm, LANES), F32)],
        compiler_params=_cparams("arbitrary", "arbitrary"),
        name="qkv_projection",
    )(x, w)


def _dsa_pro_body(x_ref, win_ref, qn_ref, kvn_ref, wuq_ref, wuqi_ref, wuk_ref,
                  qlat_ref, qidx_ref, widx_ref, ckv_ref, kidx_ref):
    proj = jnp.dot(x_ref[...], win_ref[...], preferred_element_type=F32)
    c_q = _rms_norm(proj[:, :Q_LORA], qn_ref[...]).astype(BF16)
    c_kv = _rms_norm(proj[:, Q_LORA:Q_LORA + KV_LORA], kvn_ref[...])
    ckv_ref[...] = c_kv.astype(BF16)
    kidx_ref[...] = proj[:, 768:768 + IDX_DIM].astype(BF16)
    widx_ref[...] = (proj[:, 896:1024] * (IDX_HEADS ** -0.5)).T[:IDX_HEADS]
    q = jnp.dot(c_q, wuq_ref[...], preferred_element_type=F32)
    for h in range(N_HEADS):
        qh = q[:, h * HEAD_DIM:(h + 1) * HEAD_DIM].astype(BF16)
        ql = jnp.dot(qh, wuk_ref[h], preferred_element_type=F32) * (HEAD_DIM ** -0.5)
        qlat_ref[h] = ql.astype(BF16)
    qi = jnp.dot(c_q, wuqi_ref[...], preferred_element_type=F32) * (IDX_DIM ** -0.5)
    for h in range(IDX_HEADS):
        qidx_ref[h] = qi[:, h * IDX_DIM:(h + 1) * IDX_DIM].astype(BF16)


def dsa_prologue(xb, w_in_p, q_norm, kv_norm, w_uq, w_uq_idx, w_uk, tm):
    n, d = xb.shape
    hw = N_HEADS * HEAD_DIM
    return pl.pallas_call(
        _dsa_pro_body,
        grid=(n // tm,),
        in_specs=[pl.BlockSpec((tm, d), lambda i: (i, 0)),
                  _resident((d, 1024), lambda i: (0, 0)),
                  _resident((1, Q_LORA), lambda i: (0, 0)),
                  _resident((1, KV_LORA), lambda i: (0, 0)),
                  _resident((Q_LORA, hw), lambda i: (0, 0)),
                  _resident((Q_LORA, IDX_HEADS * IDX_DIM), lambda i: (0, 0)),
                  _resident((N_HEADS, HEAD_DIM, KV_LORA), lambda i: (0, 0, 0))],
        out_specs=[pl.BlockSpec((N_HEADS, tm, KV_LORA), lambda i: (0, i, 0)),
                   pl.BlockSpec((IDX_HEADS, tm, IDX_DIM), lambda i: (0, i, 0)),
                   pl.BlockSpec((IDX_HEADS, tm), lambda i: (0, i)),
                   pl.BlockSpec((tm, KV_LORA), lambda i: (i, 0)),
                   pl.BlockSpec((tm, IDX_DIM), lambda i: (i, 0))],
        out_shape=[jax.ShapeDtypeStruct((N_HEADS, n, KV_LORA), BF16),
                   jax.ShapeDtypeStruct((IDX_HEADS, n, IDX_DIM), BF16),
                   jax.ShapeDtypeStruct((IDX_HEADS, n), F32),
                   jax.ShapeDtypeStruct((n, KV_LORA), BF16),
                   jax.ShapeDtypeStruct((n, IDX_DIM), BF16)],
        compiler_params=_cparams("arbitrary"),
        name="dsa_prologue",
    )(xb, w_in_p, q_norm, kv_norm, w_uq, w_uq_idx, w_uk)


DSA_TQ = 128
DSA_TK = 256


def _dsa_attn_body(qlat_ref, qidx_ref, widxt_ref, ckv_ref, kidx_ref, tab_ref, wuv_ref, o_ref,
                   keys_ref, keyst_ref, acc_ref, m_ref, pstar_ref,
                   s0_ref, s1_ref, p0_ref, p1_ref, a0_ref, a1_ref,
                   *, seq, topk, n_tab):
    tq, tk = DSA_TQ, DSA_TK
    i = pl.program_id(1)
    q0 = i * tq
    nk = (q0 + tq + tk - 1) // tk
    qpos = q0 + lax.broadcasted_iota(I32, (tq, tk), 0)
    col = lax.broadcasted_iota(I32, (tq, tk), 1)
    qpos_t = q0 + lax.broadcasted_iota(I32, (tk, tq), 1)
    krow_t = lax.broadcasted_iota(I32, (tk, tq), 0)

    def dup(v):
        return jnp.concatenate([v] * (tk // LANES), axis=1)

    def score_tile(j, carry):
        k0 = pl.multiple_of(j * tk, tk)
        kt = kidx_ref[pl.ds(k0, tk), :]
        rel = lax.dot_general(kt, qidx_ref[...].reshape(IDX_HEADS * tq, IDX_DIM),
                              (((1,), (1,)), ((), ())), preferred_element_type=F32)
        s = jnp.zeros((tk, tq), F32)
        for h in range(IDX_HEADS):
            s = s + jnp.maximum(rel[:, h * tq:(h + 1) * tq], 0.0) * widxt_ref[h:h + 1, :]
        bits = pltpu.bitcast(s, I32)
        key = bits ^ ((bits >> 31) & 0x7FFFFFFF)
        key = jnp.where(k0 + krow_t <= qpos_t, key, INT_MIN)
        keyst_ref[j] = key
        keys_ref[j] = key.T
        return carry

    lax.fori_loop(0, nk, score_tile, 0)

    @pl.when(nk % 2 == 1)
    def _():
        keyst_ref[nk] = jnp.full((tk, tq), INT_MIN, I32)

    def count(pred):
        def body(jj, acc):
            for j in (2 * jj, 2 * jj + 1):
                ind = jnp.where(pred(keyst_ref[j], j), 1.0, 0.0)
                parts = [ind[r * 8:(r + 1) * 8, :] for r in range(tk // 8)]
                while len(parts) > 1:
                    parts = [a + b for a, b in zip(parts[::2], parts[1::2])]
                acc = acc + parts[0]
            return acc
        acc = lax.fori_loop(0, (nk + 1) // 2, body, jnp.zeros((8, tq), F32))
        return jnp.sum(acc, axis=0, keepdims=True)

    def bit_step(t, carry):
        v, cge = carry
        cand_u = v | jnp.left_shift(jnp.int32(1), 31 - t)
        cand = cand_u ^ INT_MIN
        cnt = count(lambda key, j: key >= cand)
        take = cnt >= topk
        return jnp.where(take, cand_u, v), jnp.where(take, cnt, cge)

    v, cge = lax.fori_loop(0, 32, bit_step,
                           (jnp.zeros((1, tq), I32), jnp.zeros((1, tq), F32)))
    thr_t = v ^ INT_MIN

    pstar_ref[...] = jnp.full((1, tq), seq, I32)
    surplus = jnp.where((v != 0) & (cge > topk), 1.0, 0.0)

    @pl.when(jnp.max(surplus) > 0.0)
    def _():
        cgt = count(lambda key, j: key > thr_t)
        need = topk - cgt

        def pos_step(t, p):
            cand = p | jnp.left_shift(jnp.int32(1), (seq.bit_length() - 1) - t)
            before = count(lambda key, j: (key == thr_t) & (j * tk + krow_t < cand))
            return jnp.where(before < need, cand, p)

        p = lax.fori_loop(0, seq.bit_length(), pos_step, jnp.zeros((1, tq), I32))
        pstar_ref[...] = jnp.where(surplus > 0.0, p, seq)

    def per_row(v_t):
        return dup(jnp.broadcast_to(v_t, (LANES, tq)).T)

    thr_w = per_row(thr_t)
    pstar_w = per_row(pstar_ref[...])

    acc_ref[...] = jnp.zeros_like(acc_ref)
    m_ref[...] = jnp.full_like(m_ref, NEG)
    dq = q0 // LANES

    n_tiles = seq // tk
    qlat2d = lambda: qlat_ref[...].reshape(N_HEADS * tq, KV_LORA)

    def kv_tile(j):
        return ckv_ref[pl.ds(pl.multiple_of(jnp.clip(j, 0, n_tiles - 1) * tk, tk), tk), :]

    def logits(j):
        return lax.dot_general(qlat2d(), kv_tile(j), (((1,), (1,)), ((), ())),
                               preferred_element_type=F32)

    s_bufs, p_bufs, alpha_bufs = (s0_ref, s1_ref), (p0_ref, p1_ref), (a0_ref, a1_ref)
    s0_ref[...] = logits(0)
    p1_ref[...] = jnp.zeros(p1_ref.shape, BF16)
    a1_ref[...] = jnp.ones(a1_ref.shape, F32)

    def attn_step(j, cur):
        prv = 1 - cur
        s_ref, p_ref, alpha_ref = s_bufs[cur], p_bufs[cur], alpha_bufs[cur]
        s_bufs[prv][...] = logits(j + 1)

        kpos = j * tk + col
        key = keys_ref[jnp.minimum(j, nk - 1)]
        sel = (kpos <= qpos) & ((key > thr_w) | ((key == thr_w) & (kpos <= pstar_w)))
        maskadd = jnp.where(sel, 0.0, NEG)
        dbase = dq - j * (tk // LANES) + 1
        didx = [jnp.clip(dbase - c, 0, n_tab - 1) for c in range(tk // LANES)]
        for h in range(N_HEADS):
            rows = slice(h * tq, (h + 1) * tq)
            bias = jnp.concatenate([tab_ref[d, h] for d in didx], axis=1)
            s = s_ref[rows, :] + bias + maskadd
            s_ref[rows, :] = s
            m_old = m_ref[rows, :]
            m_new = jnp.maximum(m_old, jnp.max(s, axis=1, keepdims=True))
            alpha_ref[rows, :] = jnp.exp(m_old - m_new)
            m_ref[rows, :] = m_new

        kv_ext = jnp.concatenate([kv_tile(j - 1), jnp.ones((tk, LANES), BF16)], axis=1)
        pv = jnp.dot(p_bufs[prv][...], kv_ext, preferred_element_type=F32)
        acc_ref[...] = alpha_bufs[prv][...] * acc_ref[...] + pv

        for h in range(N_HEADS):
            rows = slice(h * tq, (h + 1) * tq)
            p_ref[rows, :] = jnp.exp(s_ref[rows, :] - m_ref[rows, :]).astype(BF16)

    def attn_pair(jj, carry):
        attn_step(2 * jj, 0)
        attn_step(2 * jj + 1, 1)
        return carry

    lax.fori_loop(0, nk // 2 + 1, attn_pair, 0)

    outs = []
    for h in range(N_HEADS):
        rows = slice(h * tq, (h + 1) * tq)
        inv_l = 1.0 / acc_ref[rows, KV_LORA:]
        o_lat = acc_ref[rows, :KV_LORA] * jnp.concatenate([inv_l] * (KV_LORA // LANES), axis=1)
        outs.append(jnp.dot(o_lat.astype(BF16), wuv_ref[h], preferred_element_type=F32))
    o_ref[...] = jnp.concatenate(outs, axis=1).astype(o_ref.dtype)


def dsa_attention(qlat, qidx, widx, ckv, kidx, tab, w_uv, bsz, seq):
    tq, tk = DSA_TQ, DSA_TK
    nq = seq // tq
    n = bsz * seq
    n_tab = tab.shape[0]
    topk = min(IDX_TOPK_MAX, seq // 4)
    body = functools.partial(_dsa_attn_body, seq=seq, topk=topk, n_tab=n_tab)
    return pl.pallas_call(
        body,
        grid=(bsz, nq),
        in_specs=[pl.BlockSpec((N_HEADS, tq, KV_LORA), lambda b, i: (0, b * nq + i, 0)),
                  pl.BlockSpec((IDX_HEADS, tq, IDX_DIM), lambda b, i: (0, b * nq + i, 0)),
                  pl.BlockSpec((IDX_HEADS, tq), lambda b, i: (0, b * nq + i)),
                  pl.BlockSpec((seq, KV_LORA), lambda b, i: (b, 0)),
                  pl.BlockSpec((seq, IDX_DIM), lambda b, i: (b, 0)),
                  _resident(tab.shape, lambda b, i: (0, 0, 0, 0)),
                  _resident((N_HEADS, KV_LORA, HEAD_DIM), lambda b, i: (0, 0, 0))],
        out_specs=pl.BlockSpec((tq, N_HEADS * HEAD_DIM), lambda b, i: (b * nq + i, 0)),
        out_shape=jax.ShapeDtypeStruct((n, N_HEADS * HEAD_DIM), BF16),
        scratch_shapes=[pltpu.VMEM((seq // tk, tq, tk), I32),
                        pltpu.VMEM((seq // tk, tk, tq), I32),
                        pltpu.VMEM((N_HEADS * tq, KV_LORA + LANES), F32),
                        pltpu.VMEM((N_HEADS * tq, 1), F32),
                        pltpu.VMEM((1, tq), I32),
                        pltpu.VMEM((N_HEADS * tq, tk), F32),
                        pltpu.VMEM((N_HEADS * tq, tk), F32),
                        pltpu.VMEM((N_HEADS * tq, tk), BF16),
                        pltpu.VMEM((N_HEADS * tq, tk), BF16),
                        pltpu.VMEM((N_HEADS * tq, 1), F32),
                        pltpu.VMEM((N_HEADS * tq, 1), F32)],
        compiler_params=_cparams("arbitrary", "arbitrary"),
        name="dsa_attention",
    )(qlat, qidx, widx, ckv, kidx, tab, w_uv)


def dsa_bias_buckets(seq):
    far = np.maximum(np.arange(seq), N_BUCKETS // 2).astype(np.float64)
    far_bucket = np.floor(np.log(far / (N_BUCKETS // 2)) / math.log(MAX_DISTANCE / (N_BUCKETS // 2))
                          * (N_BUCKETS - N_BUCKETS // 2))
    unsat = np.nonzero(far_bucket < N_BUCKETS - 1 - N_BUCKETS // 2)[0]
    n_sat = int(unsat[-1]) + 1 if unsat.size else 0
    n_tab = min(seq // LANES + 1, -(-(n_sat + LANES - 1) // LANES) + 2)
    t = np.arange(n_tab)[:, None, None]
    qi = np.arange(LANES)[None, :, None]
    ki = np.arange(LANES)[None, None, :]
    dist = np.maximum(LANES * (t - 1) + qi - ki, 0)
    return _t5_bucket(jnp.asarray(dist, I32))


def _dil_body(q_ref, kp_ref, km_ref, vp_ref, vm_ref, bias_ref, o_ref, lse_ref, *, look):
    nblk = pl.program_id(2)
    tq = q_ref.shape[2]
    ncol = look + tq
    colv = lax.broadcasted_iota(I32, (1, ncol), 1)
    first = jnp.where((nblk == 0) & (colv < look), NEG, 0.0)
    lse_ref[0, 0] = jnp.zeros(lse_ref.shape[2:], F32)
    outs = []
    for h in range(N_HEADS):
        sl = slice(h * HEAD_DIM, (h + 1) * HEAD_DIM)
        q = q_ref[0, 0, :, sl]
        k = jnp.concatenate([kp_ref[0, 0, :, sl], km_ref[0, 0, :, sl]], axis=0)
        v = jnp.concatenate([vp_ref[0, 0, :, sl], vm_ref[0, 0, :, sl]], axis=0)
        s = lax.dot_general(q, k, (((1,), (1,)), ((), ())), preferred_element_type=F32)
        s = s + bias_ref[0, h] + first
        m = jnp.max(s, axis=1, keepdims=True)
        p = jnp.exp(s - m)
        l = jnp.sum(p, axis=1, keepdims=True)
        o = jnp.dot(p.astype(BF16), v, preferred_element_type=F32) * (1.0 / l)
        outs.append(o)
        lse_ref[0, 0, :, h:h + 1] = m + jnp.log(l)
    o_ref[0, 0] = jnp.concatenate(outs, axis=1).astype(o_ref.dtype)


def dilated_branch(qkv, bias, look):
    bsz, dil, ls, _ = qkv.shape
    hw = N_HEADS * HEAD_DIM
    tq = bias.shape[2]
    nb = ls // tq
    r = tq // look
    prev = lambda part: (lambda b, c, n: (b, c, jnp.maximum(n * r - 1, 0), part))
    main = lambda part: (lambda b, c, n: (b, c, n, part))
    return pl.pallas_call(
        functools.partial(_dil_body, look=look),
        grid=(bsz, dil, nb),
        in_specs=[pl.BlockSpec((1, 1, tq, hw), main(0)),
                  pl.BlockSpec((1, 1, look, hw), prev(1)),
                  pl.BlockSpec((1, 1, tq, hw), main(1)),
                  pl.BlockSpec((1, 1, look, hw), prev(2)),
                  pl.BlockSpec((1, 1, tq, hw), main(2)),
                  _resident(bias.shape, lambda b, c, n: (0, 0, 0, 0))],
        out_specs=[pl.BlockSpec((1, 1, tq, hw), lambda b, c, n: (b, c, n, 0)),
                   pl.BlockSpec((1, 1, tq, LANES), lambda b, c, n: (b, c, n, 0))],
        out_shape=[jax.ShapeDtypeStruct((bsz, dil, ls, hw), BF16),
                   jax.ShapeDtypeStruct((bsz, dil, ls, LANES), F32)],
        compiler_params=_cparams("arbitrary", "arbitrary", "arbitrary"),
        name=f"dilated_d{dil}",
    )(qkv, qkv, qkv, qkv, qkv, bias)


def dilated_bias_buckets(tq, look, dil):
    rel = np.arange(tq)[:, None] + look - np.arange(look + tq)[None, :]
    band = (rel >= 0) & (rel <= look)
    bucket = _t5_bucket(jnp.asarray(dil * np.maximum(rel, 0), I32))
    return jnp.where(jnp.asarray(band), bucket, MASK_BUCKET)[None]


def _post_attn_body(*refs, dils, alpha):
    n_parts = len(dils)
    merge = n_parts > 1
    o_refs = refs[:n_parts]
    lse_refs = refs[n_parts:2 * n_parts] if merge else ()
    base = n_parts + len(lse_refs)
    wo_ref, x_ref, g_ref, b_ref, xo_ref, xb_ref = refs[base:base + 6]
    scratch = list(refs[base + 6:])
    tm = x_ref.shape[0]
    head = lambda h: slice(h * HEAD_DIM, (h + 1) * HEAD_DIM)
    if not merge:
        o = o_refs[0][0, 0]
    else:
        lses, o_heads = [], []
        for part, dil in enumerate(dils):
            if dil == 1:
                lses.append(lse_refs[part][0, 0])
                o_heads.append(lambda h, r=o_refs[part]: r[0, 0, :, head(h)].astype(F32))
                continue
            o_scr, l_scr = scratch.pop(0), scratch.pop(0)
            rows = tm // dil
            for c in range(dil):
                l_scr[pl.ds(c, rows, stride=dil), :] = lse_refs[part][0, c]
                for h in range(N_HEADS):
                    o_scr[h, pl.ds(c, rows, stride=dil), :] = (
                        o_refs[part][0, c, :, head(h)].astype(F32))
            lses.append(l_scr[...])
            o_heads.append(lambda h, r=o_scr: r[h])
        mx = functools.reduce(jnp.maximum, lses)
        ws = [jnp.exp(v - mx) for v in lses]
        inv = 1.0 / functools.reduce(lambda a, b: a + b, ws)
        ws = [w * inv for w in ws]
        cols = []
        for h in range(N_HEADS):
            acc = None
            for part in range(n_parts):
                term = ws[part][:, h:h + 1] * o_heads[part](h)
                acc = term if acc is None else acc + term
            cols.append(acc)
        o = jnp.concatenate(cols, axis=1).astype(BF16)
    hproj = jnp.dot(o, wo_ref[...], preferred_element_type=F32)
    y = _layer_norm(alpha * x_ref[...] + hproj, g_ref[...], b_ref[...])
    xo_ref[...] = y
    xb_ref[...] = y.astype(BF16)


def post_attention(o_parts, lse_parts, w_o, x, g, b, alpha, tm):
    n, d = x.shape
    hw = N_HEADS * HEAD_DIM
    dils = tuple(o.shape[1] for o in o_parts)
    seq = o_parts[0].shape[1] * o_parts[0].shape[2]
    tpb = seq // tm
    row = lambda width: pl.BlockSpec((tm, width), lambda i: (i, 0))
    by_class = lambda dil, width: pl.BlockSpec((1, dil, tm // dil, width),
                                               lambda i: (i // tpb, 0, i % tpb, 0))
    in_specs = ([by_class(dil, hw) for dil in dils]
                + [by_class(dil, LANES) for dil in dils[:len(lse_parts)]]
                + [_resident((hw, d), lambda i: (0, 0)), row(d),
                   _resident((1, d), lambda i: (0, 0)), _resident((1, d), lambda i: (0, 0))])
    scratch = []
    for dil in dils:
        if dil > 1:
            scratch += [pltpu.VMEM((N_HEADS, tm, HEAD_DIM), F32), pltpu.VMEM((tm, LANES), F32)]
    return pl.pallas_call(
        functools.partial(_post_attn_body, dils=dils, alpha=alpha),
        grid=(n // tm,),
        in_specs=in_specs,
        out_specs=[row(d), row(d)],
        out_shape=[jax.ShapeDtypeStruct((n, d), F32), jax.ShapeDtypeStruct((n, d), BF16)],
        scratch_shapes=scratch,
        compiler_params=_cparams("arbitrary"),
        name=f"post_attention_{len(dils)}",
    )(*o_parts, *lse_parts, w_o, x, g, b)


def _router_body(x_ref, rwt_ref, rb_ref, route_ref, gates_ref, counts_ref, carry_ref):
    tm = x_ref.shape[0]

    @pl.when(pl.program_id(0) == 0)
    def _():
        carry_ref[...] = jnp.zeros_like(carry_ref)

    logits = lax.dot_general(rwt_ref[...], x_ref[...], (((1,), (1,)), ((), ())),
                             preferred_element_type=F32,
                             precision=lax.Precision.HIGHEST)
    rows = [logits[e:e + 1, :] for e in range(N_EXPERTS)]
    mx = functools.reduce(jnp.maximum, rows)
    ex = [jnp.exp(r - mx) for r in rows]
    inv = 1.0 / functools.reduce(lambda a, b: a + b, ex)
    probs = [e * inv for e in ex]
    sel = [probs[e] + rb_ref[e] for e in range(N_EXPERTS)]

    zero_i = jnp.zeros((1, tm), I32)
    best_t = None
    for g in range(N_GROUPS):
        vals = sel[g * EXPERTS_PER_GROUP:(g + 1) * EXPERTS_PER_GROUP]
        prb = probs[g * EXPERTS_PER_GROUP:(g + 1) * EXPERTS_PER_GROUP]
        top, top_i, top_p = vals[0], zero_i, prb[0]
        for j in range(1, EXPERTS_PER_GROUP):
            c = vals[j] > top
            top = jnp.where(c, vals[j], top)
            top_i = jnp.where(c, j, top_i)
            top_p = jnp.where(c, prb[j], top_p)
        sec = jnp.full((1, tm), -jnp.inf, F32)
        sec_i, sec_p = zero_i, jnp.zeros((1, tm), F32)
        for j in range(EXPERTS_PER_GROUP):
            c = (vals[j] > sec) & (top_i != j)
            sec = jnp.where(c, vals[j], sec)
            sec_i = jnp.where(c, j, sec_i)
            sec_p = jnp.where(c, prb[j], sec_p)
        tot = top + sec
        if best_t is None:
            best_t = tot
            e0, e1 = top_i + g * EXPERTS_PER_GROUP, sec_i + g * EXPERTS_PER_GROUP
            p0, p1 = top_p, sec_p
        else:
            c = tot > best_t
            best_t = jnp.where(c, tot, best_t)
            e0 = jnp.where(c, top_i + g * EXPERTS_PER_GROUP, e0)
            e1 = jnp.where(c, sec_i + g * EXPERTS_PER_GROUP, e1)
            p0 = jnp.where(c, top_p, p0)
            p1 = jnp.where(c, sec_p, p1)
    ginv = 1.0 / (p0 + p1)
    g0, g1 = p0 * ginv, p1 * ginv
    gates_ref[...] = jnp.concatenate([g0, g1, jnp.zeros((LANES - 2, tm), F32)], axis=0).T

    assign = jnp.concatenate([jnp.where((e0 == e) | (e1 == e), 1.0, 0.0)
                              for e in range(N_EXPERTS)], axis=0).astype(BF16)
    upper = jnp.where(lax.broadcasted_iota(I32, (tm, tm), 0)
                      <= lax.broadcasted_iota(I32, (tm, tm), 1), 1.0, 0.0).astype(BF16)
    prefix = jnp.dot(assign, upper, preferred_element_type=F32) + carry_ref[...]
    rank0 = jnp.zeros((1, tm), F32)
    rank1 = jnp.zeros((1, tm), F32)
    for e in range(N_EXPERTS):
        rank0 = jnp.where(e0 == e, prefix[e:e + 1, :], rank0)
        rank1 = jnp.where(e1 == e, prefix[e:e + 1, :], rank1)
    total = prefix[:, tm - 1:tm]
    carry_ref[...] = total
    counts_ref[...] = jnp.broadcast_to(total, counts_ref.shape).astype(I32)
    route_ref[...] = jnp.concatenate(
        [e0, e1, (rank0 - 1.0).astype(I32), (rank1 - 1.0).astype(I32), jnp.zeros((4, tm), I32)],
        axis=0)


def router(x, router_wt, router_b, tm):
    n, d = x.shape
    return pl.pallas_call(
        _router_body,
        grid=(n // tm,),
        in_specs=[pl.BlockSpec((tm, d), lambda i: (i, 0)),
                  _resident((N_EXPERTS, d), lambda i: (0, 0)),
                  pl.BlockSpec(memory_space=pltpu.SMEM)],
        out_specs=[pl.BlockSpec((8, tm), lambda i: (0, i)),
                   pl.BlockSpec((tm, LANES), lambda i: (i, 0)),
                   pl.BlockSpec((N_EXPERTS, LANES), lambda i: (0, 0))],
        out_shape=[jax.ShapeDtypeStruct((8, n), I32),
                   jax.ShapeDtypeStruct((n, LANES), F32),
                   jax.ShapeDtypeStruct((N_EXPERTS, LANES), I32)],
        scratch_shapes=[pltpu.VMEM((N_EXPERTS, 1), F32)],
        compiler_params=_cparams("arbitrary"),
        name="router",
    )(x, router_wt, router_b)


EXPERT_TILE = 512
COMBINE_TILE = 256


def _row_copy(src_hbm, row, dst, dst_row, sem):
    return pltpu.make_async_copy(src_hbm.at[pl.ds(row, 1), :], dst.at[pl.ds(dst_row, 1), :], sem)


def _experts_body(te_ref, nu_ref, src_next_ref, src_first_ref, x_hbm, wg_ref, wu_ref, wd_ref,
                  ys_ref, xbuf, sem, wgb, wub, wdb):
    i = pl.program_id(0)
    n_used = nu_ref[0]
    slot = i % 2
    tmx = ys_ref.shape[0]

    def issue(src_ref, dst_slot):
        def body(r, c):
            _row_copy(x_hbm, src_ref[0, 0, r], xbuf.at[dst_slot], r, sem.at[dst_slot]).start()
            return c
        lax.fori_loop(0, tmx, body, 0, unroll=8)

    @pl.when(i == 0)
    def _():
        issue(src_first_ref, 0)

    @pl.when(i + 1 < n_used)
    def _():
        issue(src_next_ref, 1 - slot)

    @pl.when(i < n_used)
    def _():
        def wait_row(r, c):
            _row_copy(x_hbm, 0, xbuf.at[slot], r, sem.at[slot]).wait()
            return c
        lax.fori_loop(0, tmx, wait_row, 0, unroll=8)

        @pl.when((i == 0) | (te_ref[i] != te_ref[jnp.maximum(i - 1, 0)]))
        def _():
            wgb[...] = wg_ref[0, 0].astype(BF16)
            wub[...] = wu_ref[0, 0].astype(BF16)
            wdb[...] = wd_ref[0, 0].astype(BF16)

        xb = xbuf[slot].astype(BF16)
        gate = jnp.dot(xb, wgb[...], preferred_element_type=F32)
        up = jnp.dot(xb, wub[...], preferred_element_type=F32)
        hidden = (gate * jax.nn.sigmoid(gate)) * up
        ys_ref[...] = jnp.dot(hidden.astype(BF16), wdb[...], preferred_element_type=F32)

    @pl.when(i >= n_used)
    def _():
        ys_ref[...] = jnp.zeros_like(ys_ref)


def moe_experts(x, src_tiles, tile_expert, n_used, w_gate, w_up, w_down, layer):
    n, d = x.shape
    n_tiles = src_tiles.shape[0]
    tmx = EXPERT_TILE
    smem_tile = lambda index_map: pl.BlockSpec((1, 1, tmx), index_map, memory_space=pltpu.SMEM)
    by_expert = lambda shape: pl.BlockSpec((1,) + shape, lambda i, te, nu: (layer, te[i], 0, 0))
    grid_spec = pltpu.PrefetchScalarGridSpec(
        num_scalar_prefetch=2,
        grid=(n_tiles,),
        in_specs=[smem_tile(lambda i, te, nu: (jnp.minimum(i + 1, n_tiles - 1), 0, 0)),
                  smem_tile(lambda i, te, nu: (0, 0, 0)),
                  pl.BlockSpec(memory_space=pl.ANY),
                  by_expert((1, d, D_EXPERT)), by_expert((1, d, D_EXPERT)),
                  by_expert((1, D_EXPERT, d))],
        out_specs=pl.BlockSpec((tmx, d), lambda i, te, nu: (i, 0)),
        scratch_shapes=[pltpu.VMEM((2, tmx, d), F32),
                        pltpu.SemaphoreType.DMA((2,)),
                        pltpu.VMEM((d, D_EXPERT), BF16),
                        pltpu.VMEM((d, D_EXPERT), BF16),
                        pltpu.VMEM((D_EXPERT, d), BF16)])
    return pl.pallas_call(
        _experts_body,
        grid_spec=grid_spec,
        out_shape=jax.ShapeDtypeStruct((n_tiles * tmx, d), F32),
        compiler_params=_cparams("arbitrary"),
        name="moe_experts",
    )(tile_expert, n_used, src_tiles, src_tiles, x, w_gate, w_up, w_down)


def _combine_body(pos_next_ref, pos_first_ref, ys_hbm, gates_ref, x_ref, g_ref, b_ref,
                  xo_ref, xbo_ref, ybuf, sem, *, alpha):
    i = pl.program_id(0)
    n_steps = pl.num_programs(0)
    slot = i % 2
    tm = x_ref.shape[0]

    def issue(pos_ref, dst_slot):
        for k in range(2):
            def body(r, c):
                _row_copy(ys_hbm, pos_ref[0, k, r], ybuf.at[dst_slot, k], r,
                          sem.at[dst_slot]).start()
                return c
            lax.fori_loop(0, tm, body, 0, unroll=8)

    @pl.when(i == 0)
    def _():
        issue(pos_first_ref, 0)

    @pl.when(i + 1 < n_steps)
    def _():
        issue(pos_next_ref, 1 - slot)

    def wait_row(r, c):
        for k in range(2):
            _row_copy(ys_hbm, 0, ybuf.at[slot, k], r, sem.at[slot]).wait()
        return c
    lax.fori_loop(0, tm, wait_row, 0, unroll=8)

    gates = gates_ref[...]
    y = gates[:, 0:1] * ybuf[slot, 0] + gates[:, 1:2] * ybuf[slot, 1]
    out = _layer_norm(alpha * x_ref[...] + y, g_ref[...], b_ref[...])
    xo_ref[...] = out
    xbo_ref[...] = out.astype(BF16)


def moe_combine(ys, pos_tiles, gates, x, g, b, alpha):
    n, d = x.shape
    tm = COMBINE_TILE
    n_steps = n // tm
    smem_tile = lambda index_map: pl.BlockSpec((1, 2, tm), index_map, memory_space=pltpu.SMEM)
    row = lambda width: pl.BlockSpec((tm, width), lambda i: (i, 0))
    return pl.pallas_call(
        functools.partial(_combine_body, alpha=alpha),
        grid=(n_steps,),
        in_specs=[smem_tile(lambda i: (jnp.minimum(i + 1, n_steps - 1), 0, 0)),
                  smem_tile(lambda i: (0, 0, 0)),
                  pl.BlockSpec(memory_space=pl.ANY),
                  row(LANES), row(d),
                  _resident((1, d), lambda i: (0, 0)), _resident((1, d), lambda i: (0, 0))],
        out_specs=[row(d), row(d)],
        out_shape=[jax.ShapeDtypeStruct((n, d), F32), jax.ShapeDtypeStruct((n, d), BF16)],
        scratch_shapes=[pltpu.VMEM((2, 2, tm, d), F32), pltpu.SemaphoreType.DMA((2,))],
        compiler_params=_cparams("arbitrary"),
        name="moe_combine",
    )(pos_tiles, pos_tiles, ys, gates, x, g, b)


def moe_layer(x, router_wt, router_b, w_gate, w_up, w_down, layer, g, b, alpha):
    n, d = x.shape
    tmx = EXPERT_TILE
    n_tiles = (2 * n) // tmx + N_EXPERTS
    route, gates, counts = router(x, router_wt, router_b, tm=512)
    counts = counts[:, 0]
    tiles_e = (counts + tmx - 1) // tmx
    tile_end = jnp.cumsum(tiles_e)
    row_off = (tile_end - tiles_e) * tmx
    pos0 = row_off[route[0]] + route[2]
    pos1 = row_off[route[1]] + route[3]
    tok = jnp.arange(n, dtype=I32)
    src = jnp.zeros((n_tiles * tmx,), I32).at[pos0].set(tok).at[pos1].set(tok)
    tile_expert = jnp.minimum(
        jnp.searchsorted(tile_end, jnp.arange(n_tiles, dtype=I32), side="right"),
        N_EXPERTS - 1).astype(I32)
    n_used = tile_end[-1:].astype(I32)
    ys = moe_experts(x, src.reshape(n_tiles, 1, tmx), tile_expert, n_used, w_gate, w_up, w_down,
                     layer)
    tm = COMBINE_TILE
    pos_tiles = jnp.stack([pos0, pos1]).reshape(2, n // tm, tm).transpose(1, 0, 2)
    return moe_combine(ys, pos_tiles, gates, x, g, b, alpha)


def kernel(x, rel_bias, router_w, router_b, a_w_in, a_q_norm, a_kv_norm, a_w_uq, a_w_uq_idx,
           a_w_uk, a_w_uv, a_w_o, b_w_in, b_w_o, ln1_g, ln1_b, ln2_g, ln2_b,
           moe_w_gate, moe_w_up, moe_w_down):
    bsz, seq, d = x.shape
    depth = ln1_g.shape[0]
    n = bsz * seq
    hw = N_HEADS * HEAD_DIM
    alpha = (2 * depth) ** 0.25

    x = x.reshape(n, d)
    xb = x.astype(BF16)
    router_wt = router_w.T

    dsa_tab = bias_expand(dsa_bias_buckets(seq), rel_bias)
    dil_bias = []
    for window, dil in DILATED_PATTERNS:
        look = window // dil
        tq = min(2 * look, seq // dil)
        dil_bias.append(bias_expand(dilated_bias_buckets(tq, look, dil), rel_bias))

    for i in range(depth):
        j = i // 2
        if i % 2 == 0:
            w_in = a_w_in[j]
            kv_end = Q_LORA + KV_LORA
            w_in_p = jnp.concatenate(
                [w_in[:, :kv_end + IDX_DIM], jnp.zeros((d, LANES - IDX_DIM), F32),
                 w_in[:, kv_end + IDX_DIM:], jnp.zeros((d, LANES - IDX_HEADS), F32)],
                axis=1).astype(BF16)
            qlat, qidx, widx, ckv, kidx = dsa_prologue(
                xb, w_in_p, a_q_norm[j][None], a_kv_norm[j][None], a_w_uq[j].astype(BF16),
                a_w_uq_idx[j].astype(BF16), a_w_uk[j].astype(BF16), tm=256)
            o = dsa_attention(qlat, qidx, widx, ckv, kidx, dsa_tab, a_w_uv[j].astype(BF16),
                              bsz, seq)
            o_parts, lse_parts, w_o = [o.reshape(bsz, 1, seq, hw)], [], a_w_o[j]
        else:
            dils = tuple(dil for _, dil in DILATED_PATTERNS)
            qkvs = qkv_projection(xb, b_w_in[j].astype(BF16), bsz, seq, dils, tm=512, tn=1024,
                                  scaled_blocks=hw // 1024, scale=HEAD_DIM ** -0.5)
            o_parts, lse_parts = [], []
            for (window, dil), qkv, bias in zip(DILATED_PATTERNS, qkvs, dil_bias):
                o_i, lse_i = dilated_branch(qkv, bias, window // dil)
                o_parts.append(o_i)
                lse_parts.append(lse_i)
            w_o = b_w_o[j]
        x, xb = post_attention(o_parts, lse_parts, w_o.astype(BF16), x, ln1_g[i][None],
                               ln1_b[i][None], alpha, tm=256)
        x, xb = moe_layer(x, router_wt, router_b, moe_w_gate, moe_w_up, moe_w_down, i,
                          ln2_g[i][None], ln2_b[i][None], alpha)
    return x.reshape(bsz, seq, d)
```

```python
import functools
import math

import numpy as np
import jax
import jax.numpy as jnp
from jax import lax
from jax.experimental import pallas as pl
from jax.experimental.pallas import tpu as pltpu

F32 = jnp.float32
BF16 = jnp.bfloat16
I32 = jnp.int32

N_HEADS = 16
HEAD_DIM = 128
Q_LORA = 512
KV_LORA = 256
IDX_HEADS = 16
IDX_DIM = 64
IDX_TOPK_MAX = 256
DILATED_PATTERNS = ((128, 1), (512, 4), (2048, 16))
N_BUCKETS = 32
MAX_DISTANCE = 2048
N_EXPERTS = 16
N_GROUPS = 4
EXPERTS_PER_GROUP = N_EXPERTS // N_GROUPS
D_EXPERT = 512
EPS = 1e-5

LANES = 128
NEG = -1e30
INT_MIN = -(2 ** 31)
MASK_BUCKET = N_BUCKETS
VMEM_LIMIT = 56 * 1024 * 1024


def _cparams(*sem):
    return pltpu.CompilerParams(dimension_semantics=sem, vmem_limit_bytes=VMEM_LIMIT)


def _resident(shape, index_map):
    return pl.BlockSpec(shape, index_map, pipeline_mode=pl.Buffered(1))


def _t5_bucket(dist):
    n = jnp.maximum(dist, 0)
    max_exact = N_BUCKETS // 2
    nf = jnp.maximum(n, max_exact).astype(F32)
    large = max_exact + (jnp.log(nf / max_exact) / math.log(MAX_DISTANCE / max_exact)
                         * (N_BUCKETS - max_exact)).astype(I32)
    large = jnp.minimum(large, N_BUCKETS - 1)
    return jnp.where(n < max_exact, n, large)


def _layer_norm(v, g, b):
    mu = jnp.mean(v, axis=-1, keepdims=True)
    c = v - mu
    var = jnp.mean(c * c, axis=-1, keepdims=True)
    return c * lax.rsqrt(var + EPS) * g + b


def _rms_norm(v, g):
    return v * lax.rsqrt(jnp.mean(v * v, axis=-1, keepdims=True) + EPS) * g


def _bias_expand_body(bucket_ref, rb_ref, o_ref):
    bk = bucket_ref[0]
    for h in range(N_HEADS):
        acc = jnp.where(bk == MASK_BUCKET, NEG, 0.0).astype(F32)
        for b in range(N_BUCKETS):
            acc = jnp.where(bk == b, rb_ref[b, h], acc)
        o_ref[0, h] = acc


def bias_expand(bucket, rel_bias):
    n, r, c = bucket.shape
    return pl.pallas_call(
        _bias_expand_body,
        grid=(n,),
        in_specs=[pl.BlockSpec((1, r, c), lambda i: (i, 0, 0)),
                  pl.BlockSpec(memory_space=pltpu.SMEM)],
        out_specs=pl.BlockSpec((1, N_HEADS, r, c), lambda i: (i, 0, 0, 0)),
        out_shape=jax.ShapeDtypeStruct((n, N_HEADS, r, c), F32),
        compiler_params=_cparams("arbitrary"),
        name="bias_expand",
    )(bucket, rel_bias)


def _qkv_body(x_ref, w_ref, *refs, dils, scaled_blocks, scale):
    out_refs, acc_ref = refs[:len(dils)], refs[len(dils)]
    tm = x_ref.shape[0]
    acc = jnp.dot(x_ref[...], w_ref[...], preferred_element_type=F32)
    acc = acc * jnp.where(pl.program_id(0) < scaled_blocks, scale, 1.0).astype(F32)
    n_slab = acc.shape[1] // LANES
    for s in range(n_slab):
        acc_ref[s] = acc[:, s * LANES:(s + 1) * LANES]
    for o_ref, dil in zip(out_refs, dils):
        if dil == 1:
            o_ref[0, 0] = acc.astype(o_ref.dtype)
            continue
        for c in range(dil):
            cols = [acc_ref[s, pl.ds(c, tm // dil, stride=dil), :] for s in range(n_slab)]
            o_ref[0, c] = jnp.concatenate(cols, axis=1).astype(o_ref.dtype)


def qkv_projection(x, w, bsz, seq, dils, tm, tn, scaled_blocks, scale):
    m, k = x.shape
    n = w.shape[1]
    tpb = seq // tm
    out_spec = lambda dil: pl.BlockSpec((1, dil, tm // dil, tn),
                                        lambda j, i: (i // tpb, 0, i % tpb, j))
    return pl.pallas_call(
        functools.partial(_qkv_body, dils=dils, scaled_blocks=scaled_blocks, scale=scale),
        grid=(n // tn, m // tm),
        in_specs=[pl.BlockSpec((tm, k), lambda j, i: (i, 0)),
                  pl.BlockSpec((k, tn), lambda j, i: (0, j))],
        out_specs=[out_spec(dil) for dil in dils],
        out_shape=[jax.ShapeDtypeStruct((bsz, dil, seq // dil, n), BF16) for dil in dils],
        scratch_shapes=[pltpu.VMEM((tn // LANES, tm, LANES), F32)],
        compiler_params=_cparams("arbitrary", "arbitrary"),
        name="qkv_projection",
    )(x, w)


def _dsa_pro_body(x_ref, win_ref, qn_ref, kvn_ref, wuq_ref, wuqi_ref, wuk_ref,
                  qlat_ref, qidx_ref, widx_ref, ckv_ref, kidx_ref):
    proj = jnp.dot(x_ref[...], win_ref[...], preferred_element_type=F32)
    c_q = _rms_norm(proj[:, :Q_LORA], qn_ref[...]).astype(BF16)
    c_kv = _rms_norm(proj[:, Q_LORA:Q_LORA + KV_LORA], kvn_ref[...])
    ckv_ref[...] = c_kv.astype(BF16)
    kidx_ref[...] = proj[:, 768:768 + IDX_DIM].astype(BF16)
    widx_ref[...] = (proj[:, 896:1024] * (IDX_HEADS ** -0.5)).T[:IDX_HEADS]
    q = jnp.dot(c_q, wuq_ref[...], preferred_element_type=F32)
    for h in range(N_HEADS):
        qh = q[:, h * HEAD_DIM:(h + 1) * HEAD_DIM].astype(BF16)
        ql = jnp.dot(qh, wuk_ref[h], preferred_element_type=F32) * (HEAD_DIM ** -0.5)
        qlat_ref[h] = ql.astype(BF16)
    qi = jnp.dot(c_q, wuqi_ref[...], preferred_element_type=F32) * (IDX_DIM ** -0.5)
    for h in range(IDX_HEADS):
        qidx_ref[h] = qi[:, h * IDX_DIM:(h + 1) * IDX_DIM].astype(BF16)


def dsa_prologue(xb, w_in_p, q_norm, kv_norm, w_uq, w_uq_idx, w_uk, tm):
    n, d = xb.shape
    hw = N_HEADS * HEAD_DIM
    return pl.pallas_call(
        _dsa_pro_body,
        grid=(n // tm,),
        in_specs=[pl.BlockSpec((tm, d), lambda i: (i, 0)),
                  _resident((d, 1024), lambda i: (0, 0)),
                  _resident((1, Q_LORA), lambda i: (0, 0)),
                  _resident((1, KV_LORA), lambda i: (0, 0)),
                  _resident((Q_LORA, hw), lambda i: (0, 0)),
                  _resident((Q_LORA, IDX_HEADS * IDX_DIM), lambda i: (0, 0)),
                  _resident((N_HEADS, HEAD_DIM, KV_LORA), lambda i: (0, 0, 0))],
        out_specs=[pl.BlockSpec((N_HEADS, tm, KV_LORA), lambda i: (0, i, 0)),
                   pl.BlockSpec((IDX_HEADS, tm, IDX_DIM), lambda i: (0, i, 0)),
                   pl.BlockSpec((IDX_HEADS, tm), lambda i: (0, i)),
                   pl.BlockSpec((tm, KV_LORA), lambda i: (i, 0)),
                   pl.BlockSpec((tm, IDX_DIM), lambda i: (i, 0))],
        out_shape=[jax.ShapeDtypeStruct((N_HEADS, n, KV_LORA), BF16),
                   jax.ShapeDtypeStruct((IDX_HEADS, n, IDX_DIM), BF16),
                   jax.ShapeDtypeStruct((IDX_HEADS, n), F32),
                   jax.ShapeDtypeStruct((n, KV_LORA), BF16),
                   jax.ShapeDtypeStruct((n, IDX_DIM), BF16)],
        compiler_params=_cparams("arbitrary"),
        name="dsa_prologue",
    )(xb, w_in_p, q_norm, kv_norm, w_uq, w_uq_idx, w_uk)


DSA_TQ = 128
DSA_TK = 256


def _dsa_attn_body(qlat_ref, qidx_ref, widxt_ref, ckv_ref, kidx_ref, tab_ref, wuv_ref, o_ref,
                   keys_ref, keyst_ref, acc_ref, m_ref, pstar_ref,
                   s0_ref, s1_ref, p0_ref, p1_ref, a0_ref, a1_ref,
                   *, seq, topk, n_tab):
    tq, tk = DSA_TQ, DSA_TK
    i = pl.program_id(1)
    q0 = i * tq
    nk = (q0 + tq + tk - 1) // tk
    qpos = q0 + lax.broadcasted_iota(I32, (tq, tk), 0)
    col = lax.broadcasted_iota(I32, (tq, tk), 1)
    qpos_t = q0 + lax.broadcasted_iota(I32, (tk, tq), 1)
    krow_t = lax.broadcasted_iota(I32, (tk, tq), 0)

    def dup(v):
        return jnp.concatenate([v] * (tk // LANES), axis=1)

    def score_tile(j, carry):
        k0 = pl.multiple_of(j * tk, tk)
        kt = kidx_ref[pl.ds(k0, tk), :]
        rel = lax.dot_general(kt, qidx_ref[...].reshape(IDX_HEADS * tq, IDX_DIM),
                              (((1,), (1,)), ((), ())), preferred_element_type=F32)
        s = jnp.zeros((tk, tq), F32)
        for h in range(IDX_HEADS):
            s = s + jnp.maximum(rel[:, h * tq:(h + 1) * tq], 0.0) * widxt_ref[h:h + 1, :]
        bits = pltpu.bitcast(s, I32)
        key = bits ^ ((bits >> 31) & 0x7FFFFFFF)
        key = jnp.where(k0 + krow_t <= qpos_t, key, INT_MIN)
        keyst_ref[j] = key
        keys_ref[j] = key.T
        return carry

    lax.fori_loop(0, nk, score_tile, 0)

    @pl.when(nk % 2 == 1)
    def _():
        keyst_ref[nk] = jnp.full((tk, tq), INT_MIN, I32)

    def count(pred):
        def body(jj, acc):
            for j in (2 * jj, 2 * jj + 1):
                ind = jnp.where(pred(keyst_ref[j], j), 1.0, 0.0)
                parts = [ind[r * 8:(r + 1) * 8, :] for r in range(tk // 8)]
                while len(parts) > 1:
                    parts = [a + b for a, b in zip(parts[::2], parts[1::2])]
                acc = acc + parts[0]
            return acc
        acc = lax.fori_loop(0, (nk + 1) // 2, body, jnp.zeros((8, tq), F32))
        return jnp.sum(acc, axis=0, keepdims=True)

    def bit_step(t, carry):
        v, cge = carry
        cand_u = v | jnp.left_shift(jnp.int32(1), 31 - t)
        cand = cand_u ^ INT_MIN
        cnt = count(lambda key, j: key >= cand)
        take = cnt >= topk
        return jnp.where(take, cand_u, v), jnp.where(take, cnt, cge)

    v, cge = lax.fori_loop(0, 32, bit_step,
                           (jnp.zeros((1, tq), I32), jnp.zeros((1, tq), F32)))
    thr_t = v ^ INT_MIN

    pstar_ref[...] = jnp.full((1, tq), seq, I32)
    surplus = jnp.where((v != 0) & (cge > topk), 1.0, 0.0)

    @pl.when(jnp.max(surplus) > 0.0)
    def _():
        cgt = count(lambda key, j: key > thr_t)
        need = topk - cgt

        def pos_step(t, p):
            cand = p | jnp.left_shift(jnp.int32(1), (seq.bit_length() - 1) - t)
            before = count(lambda key, j: (key == thr_t) & (j * tk + krow_t < cand))
            return jnp.where(before < need, cand, p)

        p = lax.fori_loop(0, seq.bit_length(), pos_step, jnp.zeros((1, tq), I32))
        pstar_ref[...] = jnp.where(surplus > 0.0, p, seq)

    def per_row(v_t):
        return dup(jnp.broadcast_to(v_t, (LANES, tq)).T)

    thr_w = per_row(thr_t)
    pstar_w = per_row(pstar_ref[...])

    acc_ref[...] = jnp.zeros_like(acc_ref)
    m_ref[...] = jnp.full_like(m_ref, NEG)
    dq = q0 // LANES

    n_tiles = seq // tk
    qlat2d = lambda: qlat_ref[...].reshape(N_HEADS * tq, KV_LORA)

    def kv_tile(j):
        return ckv_ref[pl.ds(pl.multiple_of(jnp.clip(j, 0, n_tiles - 1) * tk, tk), tk), :]

    def logits(j):
        return lax.dot_general(qlat2d(), kv_tile(j), (((1,), (1,)), ((), ())),
                               preferred_element_type=F32)

    s_bufs, p_bufs, alpha_bufs = (s0_ref, s1_ref), (p0_ref, p1_ref), (a0_ref, a1_ref)
    s0_ref[...] = logits(0)
    p1_ref[...] = jnp.zeros(p1_ref.shape, BF16)
    a1_ref[...] = jnp.ones(a1_ref.shape, F32)

    def attn_step(j, cur):
        prv = 1 - cur
        s_ref, p_ref, alpha_ref = s_bufs[cur], p_bufs[cur], alpha_bufs[cur]
        s_bufs[prv][...] = logits(j + 1)

        kpos = j * tk + col
        key = keys_ref[jnp.minimum(j, nk - 1)]
        sel = (kpos <= qpos) & ((key > thr_w) | ((key == thr_w) & (kpos <= pstar_w)))
        maskadd = jnp.where(sel, 0.0, NEG)
        dbase = dq - j * (tk // LANES) + 1
        didx = [jnp.clip(dbase - c, 0, n_tab - 1) for c in range(tk // LANES)]
        for h in range(N_HEADS):
            rows = slice(h * tq, (h + 1) * tq)
            bias = jnp.concatenate([tab_ref[d, h] for d in didx], axis=1)
            s = s_ref[rows, :] + bias + maskadd
            s_ref[rows, :] = s
            m_old = m_ref[rows, :]
            m_new = jnp.maximum(m_old, jnp.max(s, axis=1, keepdims=True))
            alpha_ref[rows, :] = jnp.exp(m_old - m_new)
            m_ref[rows, :] = m_new

        kv_ext = jnp.concatenate([kv_tile(j - 1), jnp.ones((tk, LANES), BF16)], axis=1)
        pv = jnp.dot(p_bufs[prv][...], kv_ext, preferred_element_type=F32)
        acc_ref[...] = alpha_bufs[prv][...] * acc_ref[...] + pv

        for h in range(N_HEADS):
            rows = slice(h * tq, (h + 1) * tq)
            p_ref[rows, :] = jnp.exp(s_ref[rows, :] - m_ref[rows, :]).astype(BF16)

    def attn_pair(jj, carry):
        attn_step(2 * jj, 0)
        attn_step(2 * jj + 1, 1)
        return carry

    lax.fori_loop(0, nk // 2 + 1, attn_pair, 0)

    outs = []
    for h in range(N_HEADS):
        rows = slice(h * tq, (h + 1) * tq)
        inv_l = 1.0 / acc_ref[rows, KV_LORA:]
        o_lat = acc_ref[rows, :KV_LORA] * jnp.concatenate([inv_l] * (KV_LORA // LANES), axis=1)
        outs.append(jnp.dot(o_lat.astype(BF16), wuv_ref[h], preferred_element_type=F32))
    o_ref[...] = jnp.concatenate(outs, axis=1).astype(o_ref.dtype)


def dsa_attention(qlat, qidx, widx, ckv, kidx, tab, w_uv, bsz, seq):
    tq, tk = DSA_TQ, DSA_TK
    nq = seq // tq
    n = bsz * seq
    n_tab = tab.shape[0]
    topk = min(IDX_TOPK_MAX, seq // 4)
    body = functools.partial(_dsa_attn_body, seq=seq, topk=topk, n_tab=n_tab)
    return pl.pallas_call(
        body,
        grid=(bsz, nq),
        in_specs=[pl.BlockSpec((N_HEADS, tq, KV_LORA), lambda b, i: (0, b * nq + i, 0)),
                  pl.BlockSpec((IDX_HEADS, tq, IDX_DIM), lambda b, i: (0, b * nq + i, 0)),
                  pl.BlockSpec((IDX_HEADS, tq), lambda b, i: (0, b * nq + i)),
                  pl.BlockSpec((seq, KV_LORA), lambda b, i: (b, 0)),
                  pl.BlockSpec((seq, IDX_DIM), lambda b, i: (b, 0)),
                  _resident(tab.shape, lambda b, i: (0, 0, 0, 0)),
                  _resident((N_HEADS, KV_LORA, HEAD_DIM), lambda b, i: (0, 0, 0))],
        out_specs=pl.BlockSpec((tq, N_HEADS * HEAD_DIM), lambda b, i: (b * nq + i, 0)),
        out_shape=jax.ShapeDtypeStruct((n, N_HEADS * HEAD_DIM), BF16),
        scratch_shapes=[pltpu.VMEM((seq // tk, tq, tk), I32),
                        pltpu.VMEM((seq // tk, tk, tq), I32),
                        pltpu.VMEM((N_HEADS * tq, KV_LORA + LANES), F32),
                        pltpu.VMEM((N_HEADS * tq, 1), F32),
                        pltpu.VMEM((1, tq), I32),
                        pltpu.VMEM((N_HEADS * tq, tk), F32),
                        pltpu.VMEM((N_HEADS * tq, tk), F32),
                        pltpu.VMEM((N_HEADS * tq, tk), BF16),
                        pltpu.VMEM((N_HEADS * tq, tk), BF16),
                        pltpu.VMEM((N_HEADS * tq, 1), F32),
                        pltpu.VMEM((N_HEADS * tq, 1), F32)],
        compiler_params=_cparams("arbitrary", "arbitrary"),
        name="dsa_attention",
    )(qlat, qidx, widx, ckv, kidx, tab, w_uv)


def dsa_bias_buckets(seq):
    far = np.maximum(np.arange(seq), N_BUCKETS // 2).astype(np.float64)
    far_bucket = np.floor(np.log(far / (N_BUCKETS // 2)) / math.log(MAX_DISTANCE / (N_BUCKETS // 2))
                          * (N_BUCKETS - N_BUCKETS // 2))
    unsat = np.nonzero(far_bucket < N_BUCKETS - 1 - N_BUCKETS // 2)[0]
    n_sat = int(unsat[-1]) + 1 if unsat.size else 0
    n_tab = min(seq // LANES + 1, -(-(n_sat + LANES - 1) // LANES) + 2)
    t = np.arange(n_tab)[:, None, None]
    qi = np.arange(LANES)[None, :, None]
    ki = np.arange(LANES)[None, None, :]
    dist = np.maximum(LANES * (t - 1) + qi - ki, 0)
    return _t5_bucket(jnp.asarray(dist, I32))


def _dil_body(q_ref, kp_ref, km_ref, vp_ref, vm_ref, bias_ref, o_ref, lse_ref, *, look):
    nblk = pl.program_id(2)
    tq = q_ref.shape[2]
    ncol = look + tq
    colv = lax.broadcasted_iota(I32, (1, ncol), 1)
    first = jnp.where((nblk == 0) & (colv < look), NEG, 0.0)
    lse_ref[0, 0] = jnp.zeros(lse_ref.shape[2:], F32)
    outs = []
    for h in range(N_HEADS):
        sl = slice(h * HEAD_DIM, (h + 1) * HEAD_DIM)
        q = q_ref[0, 0, :, sl]
        k = jnp.concatenate([kp_ref[0, 0, :, sl], km_ref[0, 0, :, sl]], axis=0)
        v = jnp.concatenate([vp_ref[0, 0, :, sl], vm_ref[0, 0, :, sl]], axis=0)
        s = lax.dot_general(q, k, (((1,), (1,)), ((), ())), preferred_element_type=F32)
        s = s + bias_ref[0, h] + first
        m = jnp.max(s, axis=1, keepdims=True)
        p = jnp.exp(s - m)
        l = jnp.sum(p, axis=1, keepdims=True)
        o = jnp.dot(p.astype(BF16), v, preferred_element_type=F32) * (1.0 / l)
        outs.append(o)
        lse_ref[0, 0, :, h:h + 1] = m + jnp.log(l)
    o_ref[0, 0] = jnp.concatenate(outs, axis=1).astype(o_ref.dtype)


def dilated_branch(qkv, bias, look):
    bsz, dil, ls, _ = qkv.shape
    hw = N_HEADS * HEAD_DIM
    tq = bias.shape[2]
    nb = ls // tq
    r = tq // look
    prev = lambda part: (lambda b, c, n: (b, c, jnp.maximum(n * r - 1, 0), part))
    main = lambda part: (lambda b, c, n: (b, c, n, part))
    return pl.pallas_call(
        functools.partial(_dil_body, look=look),
        grid=(bsz, dil, nb),
        in_specs=[pl.BlockSpec((1, 1, tq, hw), main(0)),
                  pl.BlockSpec((1, 1, look, hw), prev(1)),
                  pl.BlockSpec((1, 1, tq, hw), main(1)),
                  pl.BlockSpec((1, 1, look, hw), prev(2)),
                  pl.BlockSpec((1, 1, tq, hw), main(2)),
                  _resident(bias.shape, lambda b, c, n: (0, 0, 0, 0))],
        out_specs=[pl.BlockSpec((1, 1, tq, hw), lambda b, c, n: (b, c, n, 0)),
                   pl.BlockSpec((1, 1, tq, LANES), lambda b, c, n: (b, c, n, 0))],
        out_shape=[jax.ShapeDtypeStruct((bsz, dil, ls, hw), BF16),
                   jax.ShapeDtypeStruct((bsz, dil, ls, LANES), F32)],
        compiler_params=_cparams("arbitrary", "arbitrary", "arbitrary"),
        name=f"dilated_d{dil}",
    )(qkv, qkv, qkv, qkv, qkv, bias)


def dilated_bias_buckets(tq, look, dil):
    rel = np.arange(tq)[:, None] + look - np.arange(look + tq)[None, :]
    band = (rel >= 0) & (rel <= look)
    bucket = _t5_bucket(jnp.asarray(dil * np.maximum(rel, 0), I32))
    return jnp.where(jnp.asarray(band), bucket, MASK_BUCKET)[None]


def _post_attn_body(*refs, dils, alpha):
    n_parts = len(dils)
    merge = n_parts > 1
    o_refs = refs[:n_parts]
    lse_refs = refs[n_parts:2 * n_parts] if merge else ()
    base = n_parts + len(lse_refs)
    wo_ref, x_ref, g_ref, b_ref, xo_ref, xb_ref = refs[base:base + 6]
    scratch = list(refs[base + 6:])
    tm = x_ref.shape[0]
    head = lambda h: slice(h * HEAD_DIM, (h + 1) * HEAD_DIM)
    if not merge:
        o = o_refs[0][0, 0]
    else:
        lses, o_heads = [], []
        for part, dil in enumerate(dils):
            if dil == 1:
                lses.append(lse_refs[part][0, 0])
                o_heads.append(lambda h, r=o_refs[part]: r[0, 0, :, head(h)].astype(F32))
                continue
            o_scr, l_scr = scratch.pop(0), scratch.pop(0)
            rows = tm // dil
            for c in range(dil):
                l_scr[pl.ds(c, rows, stride=dil), :] = lse_refs[part][0, c]
                for h in range(N_HEADS):
                    o_scr[h, pl.ds(c, rows, stride=dil), :] = (
                        o_refs[part][0, c, :, head(h)].astype(F32))
            lses.append(l_scr[...])
            o_heads.append(lambda h, r=o_scr: r[h])
        mx = functools.reduce(jnp.maximum, lses)
        ws = [jnp.exp(v - mx) for v in lses]
        inv = 1.0 / functools.reduce(lambda a, b: a + b, ws)
        ws = [w * inv for w in ws]
        cols = []
        for h in range(N_HEADS):
            acc = None
            for part in range(n_parts):
                term = ws[part][:, h:h + 1] * o_heads[part](h)
                acc = term if acc is None else acc + term
            cols.append(acc)
        o = jnp.concatenate(cols, axis=1).astype(BF16)
    hproj = jnp.dot(o, wo_ref[...], preferred_element_type=F32)
    y = _layer_norm(alpha * x_ref[...] + hproj, g_ref[...], b_ref[...])
    xo_ref[...] = y
    xb_ref[...] = y.astype(BF16)


def post_attention(o_parts, lse_parts, w_o, x, g, b, alpha, tm):
    n, d = x.shape
    hw = N_HEADS * HEAD_DIM
    dils = tuple(o.shape[1] for o in o_parts)
    seq = o_parts[0].shape[1] * o_parts[0].shape[2]
    tpb = seq // tm
    row = lambda width: pl.BlockSpec((tm, width), lambda i: (i, 0))
    by_class = lambda dil, width: pl.BlockSpec((1, dil, tm // dil, width),
                                               lambda i: (i // tpb, 0, i % tpb, 0))
    in_specs = ([by_class(dil, hw) for dil in dils]
                + [by_class(dil, LANES) for dil in dils[:len(lse_parts)]]
                + [_resident((hw, d), lambda i: (0, 0)), row(d),
                   _resident((1, d), lambda i: (0, 0)), _resident((1, d), lambda i: (0, 0))])
    scratch = []
    for dil in dils:
        if dil > 1:
            scratch += [pltpu.VMEM((N_HEADS, tm, HEAD_DIM), F32), pltpu.VMEM((tm, LANES), F32)]
    return pl.pallas_call(
        functools.partial(_post_attn_body, dils=dils, alpha=alpha),
        grid=(n // tm,),
        in_specs=in_specs,
        out_specs=[row(d), row(d)],
        out_shape=[jax.ShapeDtypeStruct((n, d), F32), jax.ShapeDtypeStruct((n, d), BF16)],
        scratch_shapes=scratch,
        compiler_params=_cparams("arbitrary"),
        name=f"post_attention_{len(dils)}",
    )(*o_parts, *lse_parts, w_o, x, g, b)


def _router_body(x_ref, rwt_ref, rb_ref, route_ref, gates_ref, counts_ref, carry_ref):
    tm = x_ref.shape[0]

    @pl.when(pl.program_id(0) == 0)
    def _():
        carry_ref[...] = jnp.zeros_like(carry_ref)

    logits = lax.dot_general(rwt_ref[...], x_ref[...], (((1,), (1,)), ((), ())),
                             preferred_element_type=F32,
                             precision=lax.Precision.HIGHEST)
    rows = [logits[e:e + 1, :] for e in range(N_EXPERTS)]
    mx = functools.reduce(jnp.maximum, rows)
    ex = [jnp.exp(r - mx) for r in rows]
    inv = 1.0 / functools.reduce(lambda a, b: a + b, ex)
    probs = [e * inv for e in ex]
    sel = [probs[e] + rb_ref[e] for e in range(N_EXPERTS)]

    zero_i = jnp.zeros((1, tm), I32)
    best_t = None
    for g in range(N_GROUPS):
        vals = sel[g * EXPERTS_PER_GROUP:(g + 1) * EXPERTS_PER_GROUP]
        prb = probs[g * EXPERTS_PER_GROUP:(g + 1) * EXPERTS_PER_GROUP]
        top, top_i, top_p = vals[0], zero_i, prb[0]
        for j in range(1, EXPERTS_PER_GROUP):
            c = vals[j] > top
            top = jnp.where(c, vals[j], top)
            top_i = jnp.where(c, j, top_i)
            top_p = jnp.where(c, prb[j], top_p)
        sec = jnp.full((1, tm), -jnp.inf, F32)
        sec_i, sec_p = zero_i, jnp.zeros((1, tm), F32)
        for j in range(EXPERTS_PER_GROUP):
            c = (vals[j] > sec) & (top_i != j)
            sec = jnp.where(c, vals[j], sec)
            sec_i = jnp.where(c, j, sec_i)
            sec_p = jnp.where(c, prb[j], sec_p)
        tot = top + sec
        if best_t is None:
            best_t = tot
            e0, e1 = top_i + g * EXPERTS_PER_GROUP, sec_i + g * EXPERTS_PER_GROUP
            p0, p1 = top_p, sec_p
        else:
            c = tot > best_t
            best_t = jnp.where(c, tot, best_t)
            e0 = jnp.where(c, top_i + g * EXPERTS_PER_GROUP, e0)
            e1 = jnp.where(c, sec_i + g * EXPERTS_PER_GROUP, e1)
            p0 = jnp.where(c, top_p, p0)
            p1 = jnp.where(c, sec_p, p1)
    ginv = 1.0 / (p0 + p1)
    g0, g1 = p0 * ginv, p1 * ginv
    gates_ref[...] = jnp.concatenate([g0, g1, jnp.zeros((LANES - 2, tm), F32)], axis=0).T

    assign = jnp.concatenate([jnp.where((e0 == e) | (e1 == e), 1.0, 0.0)
                              for e in range(N_EXPERTS)], axis=0).astype(BF16)
    upper = jnp.where(lax.broadcasted_iota(I32, (tm, tm), 0)
                      <= lax.broadcasted_iota(I32, (tm, tm), 1), 1.0, 0.0).astype(BF16)
    prefix = jnp.dot(assign, upper, preferred_element_type=F32) + carry_ref[...]
    rank0 = jnp.zeros((1, tm), F32)
    rank1 = jnp.zeros((1, tm), F32)
    for e in range(N_EXPERTS):
        rank0 = jnp.where(e0 == e, prefix[e:e + 1, :], rank0)
        rank1 = jnp.where(e1 == e, prefix[e:e + 1, :], rank1)
    total = prefix[:, tm - 1:tm]
    carry_ref[...] = total
    counts_ref[...] = jnp.broadcast_to(total, counts_ref.shape).astype(I32)
    route_ref[...] = jnp.concatenate(
        [e0, e1, (rank0 - 1.0).astype(I32), (rank1 - 1.0).astype(I32), jnp.zeros((4, tm), I32)],
        axis=0)


def router(x, router_wt, router_b, tm):
    n, d = x.shape
    return pl.pallas_call(
        _router_body,
        grid=(n // tm,),
        in_specs=[pl.BlockSpec((tm, d), lambda i: (i, 0)),
                  _resident((N_EXPERTS, d), lambda i: (0, 0)),
                  pl.BlockSpec(memory_space=pltpu.SMEM)],
        out_specs=[pl.BlockSpec((8, tm), lambda i: (0, i)),
                   pl.BlockSpec((tm, LANES), lambda i: (i, 0)),
                   pl.BlockSpec((N_EXPERTS, LANES), lambda i: (0, 0))],
        out_shape=[jax.ShapeDtypeStruct((8, n), I32),
                   jax.ShapeDtypeStruct((n, LANES), F32),
                   jax.ShapeDtypeStruct((N_EXPERTS, LANES), I32)],
        scratch_shapes=[pltpu.VMEM((N_EXPERTS, 1), F32)],
        compiler_params=_cparams("arbitrary"),
        name="router",
    )(x, router_wt, router_b)


EXPERT_TILE = 512
COMBINE_TILE = 256


def _row_copy(src_hbm, row, dst, dst_row, sem):
    return pltpu.make_async_copy(src_hbm.at[pl.ds(row, 1), :], dst.at[pl.ds(dst_row, 1), :], sem)


def _experts_body(te_ref, nu_ref, src_next_ref, src_first_ref, x_hbm, wg_ref, wu_ref, wd_ref,
                  ys_ref, xbuf, sem, wgb, wub, wdb):
    i = pl.program_id(0)
    n_used = nu_ref[0]
    slot = i % 2
    tmx = ys_ref.shape[0]

    def issue(src_ref, dst_slot):
        for r in range(tmx):
            _row_copy(x_hbm, src_ref[0, 0, r], xbuf.at[dst_slot], r, sem.at[dst_slot]).start()

    @pl.when(i == 0)
    def _():
        issue(src_first_ref, 0)

    @pl.when(i + 1 < n_used)
    def _():
        issue(src_next_ref, 1 - slot)

    @pl.when(i < n_used)
    def _():
        def wait_row(r, c):
            _row_copy(x_hbm, 0, xbuf.at[slot], r, sem.at[slot]).wait()
            return c
        lax.fori_loop(0, tmx, wait_row, 0, unroll=8)

        @pl.when((i == 0) | (te_ref[i] != te_ref[jnp.maximum(i - 1, 0)]))
        def _():
            wgb[...] = wg_ref[0, 0].astype(BF16)
            wub[...] = wu_ref[0, 0].astype(BF16)
            wdb[...] = wd_ref[0, 0].astype(BF16)

        xb = xbuf[slot].astype(BF16)
        gate = jnp.dot(xb, wgb[...], preferred_element_type=F32)
        up = jnp.dot(xb, wub[...], preferred_element_type=F32)
        hidden = (gate * jax.nn.sigmoid(gate)) * up
        ys_ref[...] = jnp.dot(hidden.astype(BF16), wdb[...], preferred_element_type=F32)

    @pl.when(i >= n_used)
    def _():
        ys_ref[...] = jnp.zeros_like(ys_ref)


def moe_experts(x, src_tiles, tile_expert, n_used, w_gate, w_up, w_down, layer):
    n, d = x.shape
    n_tiles = src_tiles.shape[0]
    tmx = EXPERT_TILE
    smem_tile = lambda index_map: pl.BlockSpec((1, 1, tmx), index_map, memory_space=pltpu.SMEM)
    by_expert = lambda shape: pl.BlockSpec((1,) + shape, lambda i, te, nu: (layer, te[i], 0, 0))
    grid_spec = pltpu.PrefetchScalarGridSpec(
        num_scalar_prefetch=2,
        grid=(n_tiles,),
        in_specs=[smem_tile(lambda i, te, nu: (jnp.minimum(i + 1, n_tiles - 1), 0, 0)),
                  smem_tile(lambda i, te, nu: (0, 0, 0)),
                  pl.BlockSpec(memory_space=pl.ANY),
                  by_expert((1, d, D_EXPERT)), by_expert((1, d, D_EXPERT)),
                  by_expert((1, D_EXPERT, d))],
        out_specs=pl.BlockSpec((tmx, d), lambda i, te, nu: (i, 0)),
        scratch_shapes=[pltpu.VMEM((2, tmx, d), F32),
                        pltpu.SemaphoreType.DMA((2,)),
                        pltpu.VMEM((d, D_EXPERT), BF16),
                        pltpu.VMEM((d, D_EXPERT), BF16),
                        pltpu.VMEM((D_EXPERT, d), BF16)])
    return pl.pallas_call(
        _experts_body,
        grid_spec=grid_spec,
        out_shape=jax.ShapeDtypeStruct((n_tiles * tmx, d), F32),
        compiler_params=_cparams("arbitrary"),
        name="moe_experts",
    )(tile_expert, n_used, src_tiles, src_tiles, x, w_gate, w_up, w_down)


def _combine_body(pos_next_ref, pos_first_ref, ys_hbm, gates_ref, x_ref, g_ref, b_ref,
                  xo_ref, xbo_ref, ybuf, sem, *, alpha):
    i = pl.program_id(0)
    n_steps = pl.num_programs(0)
    slot = i % 2
    tm = x_ref.shape[0]

    def issue(pos_ref, dst_slot):
        for k in range(2):
            for r in range(tm):
                _row_copy(ys_hbm, pos_ref[0, k, r], ybuf.at[dst_slot, k], r,
                          sem.at[dst_slot]).start()

    @pl.when(i == 0)
    def _():
        issue(pos_first_ref, 0)

    @pl.when(i + 1 < n_steps)
    def _():
        issue(pos_next_ref, 1 - slot)

    def wait_row(r, c):
        for k in range(2):
            _row_copy(ys_hbm, 0, ybuf.at[slot, k], r, sem.at[slot]).wait()
        return c
    lax.fori_loop(0, tm, wait_row, 0, unroll=8)

    gates = gates_ref[...]
    y = gates[:, 0:1] * ybuf[slot, 0] + gates[:, 1:2] * ybuf[slot, 1]
    out = _layer_norm(alpha * x_ref[...] + y, g_ref[...], b_ref[...])
    xo_ref[...] = out
    xbo_ref[...] = out.astype(BF16)


def moe_combine(ys, pos_tiles, gates, x, g, b, alpha):
    n, d = x.shape
    tm = COMBINE_TILE
    n_steps = n // tm
    smem_tile = lambda index_map: pl.BlockSpec((1, 2, tm), index_map, memory_space=pltpu.SMEM)
    row = lambda width: pl.BlockSpec((tm, width), lambda i: (i, 0))
    return pl.pallas_call(
        functools.partial(_combine_body, alpha=alpha),
        grid=(n_steps,),
        in_specs=[smem_tile(lambda i: (jnp.minimum(i + 1, n_steps - 1), 0, 0)),
                  smem_tile(lambda i: (0, 0, 0)),
                  pl.BlockSpec(memory_space=pl.ANY),
                  row(LANES), row(d),
                  _resident((1, d), lambda i: (0, 0)), _resident((1, d), lambda i: (0, 0))],
        out_specs=[row(d), row(d)],
        out_shape=[jax.ShapeDtypeStruct((n, d), F32), jax.ShapeDtypeStruct((n, d), BF16)],
        scratch_shapes=[pltpu.VMEM((2, 2, tm, d), F32), pltpu.SemaphoreType.DMA((2,))],
        compiler_params=_cparams("arbitrary"),
        name="moe_combine",
    )(pos_tiles, pos_tiles, ys, gates, x, g, b)


def moe_layer(x, router_wt, router_b, w_gate, w_up, w_down, layer, g, b, alpha):
    n, d = x.shape
    tmx = EXPERT_TILE
    n_tiles = (2 * n) // tmx + N_EXPERTS
    route, gates, counts = router(x, router_wt, router_b, tm=512)
    counts = counts[:, 0]
    tiles_e = (counts + tmx - 1) // tmx
    tile_end = jnp.cumsum(tiles_e)
    row_off = (tile_end - tiles_e) * tmx
    pos0 = row_off[route[0]] + route[2]
    pos1 = row_off[route[1]] + route[3]
    tok = jnp.arange(n, dtype=I32)
    src = jnp.zeros((n_tiles * tmx,), I32).at[pos0].set(tok).at[pos1].set(tok)
    tile_expert = jnp.minimum(
        jnp.searchsorted(tile_end, jnp.arange(n_tiles, dtype=I32), side="right"),
        N_EXPERTS - 1).astype(I32)
    n_used = tile_end[-1:].astype(I32)
    ys = moe_experts(x, src.reshape(n_tiles, 1, tmx), tile_expert, n_used, w_gate, w_up, w_down,
                     layer)
    tm = COMBINE_TILE
    pos_tiles = jnp.stack([pos0, pos1]).reshape(2, n // tm, tm).transpose(1, 0, 2)
    return moe_combine(ys, pos_tiles, gates, x, g, b, alpha)


def kernel(x, rel_bias, router_w, router_b, a_w_in, a_q_norm, a_kv_norm, a_w_uq, a_w_uq_idx,
           a_w_uk, a_w_uv, a_w_o, b_w_in, b_w_o, ln1_g, ln1_b, ln2_g, ln2_b,
           moe_w_gate, moe_w_up, moe_w_down):
    bsz, seq, d = x.shape
    depth = ln1_g.shape[0]
    n = bsz * seq
    hw = N_HEADS * HEAD_DIM
    alpha = (2 * depth) ** 0.25

    x = x.reshape(n, d)
    xb = x.astype(BF16)
    router_wt = router_w.T

    dsa_tab = bias_expand(dsa_bias_buckets(seq), rel_bias)
    dil_bias = []
    for window, dil in DILATED_PATTERNS:
        look = window // dil
        tq = min(2 * look, seq // dil)
        dil_bias.append(bias_expand(dilated_bias_buckets(tq, look, dil), rel_bias))

    for i in range(depth):
        j = i // 2
        if i % 2 == 0:
            w_in = a_w_in[j]
            kv_end = Q_LORA + KV_LORA
            w_in_p = jnp.concatenate(
                [w_in[:, :kv_end + IDX_DIM], jnp.zeros((d, LANES - IDX_DIM), F32),
                 w_in[:, kv_end + IDX_DIM:], jnp.zeros((d, LANES - IDX_HEADS), F32)],
                axis=1).astype(BF16)
            qlat, qidx, widx, ckv, kidx = dsa_prologue(
                xb, w_in_p, a_q_norm[j][None], a_kv_norm[j][None], a_w_uq[j].astype(BF16),
                a_w_uq_idx[j].astype(BF16), a_w_uk[j].astype(BF16), tm=256)
            o = dsa_attention(qlat, qidx, widx, ckv, kidx, dsa_tab, a_w_uv[j].astype(BF16),
                              bsz, seq)
            o_parts, lse_parts, w_o = [o.reshape(bsz, 1, seq, hw)], [], a_w_o[j]
        else:
            dils = tuple(dil for _, dil in DILATED_PATTERNS)
            qkvs = qkv_projection(xb, b_w_in[j].astype(BF16), bsz, seq, dils, tm=512, tn=1024,
                                  scaled_blocks=hw // 1024, scale=HEAD_DIM ** -0.5)
            o_parts, lse_parts = [], []
            for (window, dil), qkv, bias in zip(DILATED_PATTERNS, qkvs, dil_bias):
                o_i, lse_i = dilated_branch(qkv, bias, window // dil)
                o_parts.append(o_i)
                lse_parts.append(lse_i)
            w_o = b_w_o[j]
        x, xb = post_attention(o_parts, lse_parts, w_o.astype(BF16), x, ln1_g[i][None],
                               ln1_b[i][None], alpha, tm=256)
        x, xb = moe_layer(x, router_wt, router_b, moe_w_gate, moe_w_up, moe_w_down, i,
                          ln2_g[i][None], ln2_b[i][None], alpha)
    return x.reshape(bsz, seq, d)
```

```python
import functools
import math

import numpy as np
import jax
import jax.numpy as jnp
from jax import lax
from jax.experimental import pallas as pl
from jax.experimental.pallas import tpu as pltpu

F32 = jnp.float32
BF16 = jnp.bfloat16
I32 = jnp.int32

N_HEADS = 16
HEAD_DIM = 128
Q_LORA = 512
KV_LORA = 256
IDX_HEADS = 16
IDX_DIM = 64
IDX_TOPK_MAX = 256
DILATED_PATTERNS = ((128, 1), (512, 4), (2048, 16))
N_BUCKETS = 32
MAX_DISTANCE = 2048
N_EXPERTS = 16
N_GROUPS = 4
EXPERTS_PER_GROUP = N_EXPERTS // N_GROUPS
D_EXPERT = 512
EPS = 1e-5

LANES = 128
NEG = -1e30
INT_MIN = -(2 ** 31)
MASK_BUCKET = N_BUCKETS
VMEM_LIMIT = 56 * 1024 * 1024


def _cparams(*sem):
    return pltpu.CompilerParams(dimension_semantics=sem, vmem_limit_bytes=VMEM_LIMIT)


def _resident(shape, index_map):
    return pl.BlockSpec(shape, index_map, pipeline_mode=pl.Buffered(1))


def _t5_bucket(dist):
    n = jnp.maximum(dist, 0)
    max_exact = N_BUCKETS // 2
    nf = jnp.maximum(n, max_exact).astype(F32)
    large = max_exact + (jnp.log(nf / max_exact) / math.log(MAX_DISTANCE / max_exact)
                         * (N_BUCKETS - max_exact)).astype(I32)
    large = jnp.minimum(large, N_BUCKETS - 1)
    return jnp.where(n < max_exact, n, large)


def _layer_norm(v, g, b):
    mu = jnp.mean(v, axis=-1, keepdims=True)
    c = v - mu
    var = jnp.mean(c * c, axis=-1, keepdims=True)
    return c * lax.rsqrt(var + EPS) * g + b


def _rms_norm(v, g):
    return v * lax.rsqrt(jnp.mean(v * v, axis=-1, keepdims=True) + EPS) * g


def _bias_expand_body(bucket_ref, rb_ref, o_ref):
    bk = bucket_ref[0]
    for h in range(N_HEADS):
        acc = jnp.where(bk == MASK_BUCKET, NEG, 0.0).astype(F32)
        for b in range(N_BUCKETS):
            acc = jnp.where(bk == b, rb_ref[b, h], acc)
        o_ref[0, h] = acc


def bias_expand(bucket, rel_bias):
    n, r, c = bucket.shape
    return pl.pallas_call(
        _bias_expand_body,
        grid=(n,),
        in_specs=[pl.BlockSpec((1, r, c), lambda i: (i, 0, 0)),
                  pl.BlockSpec(memory_space=pltpu.SMEM)],
        out_specs=pl.BlockSpec((1, N_HEADS, r, c), lambda i: (i, 0, 0, 0)),
        out_shape=jax.ShapeDtypeStruct((n, N_HEADS, r, c), F32),
        compiler_params=_cparams("arbitrary"),
        name="bias_expand",
    )(bucket, rel_bias)


def _qkv_body(x_ref, w_ref, *refs, dils, scaled_blocks, scale):
    out_refs, acc_ref = refs[:len(dils)], refs[len(dils)]
    tm = x_ref.shape[0]
    acc = jnp.dot(x_ref[...], w_ref[...], preferred_element_type=F32)
    acc = acc * jnp.where(pl.program_id(0) < scaled_blocks, scale, 1.0).astype(F32)
    n_slab = acc.shape[1] // LANES
    for s in range(n_slab):
        acc_ref[s] = acc[:, s * LANES:(s + 1) * LANES]
    for o_ref, dil in zip(out_refs, dils):
        if dil == 1:
            o_ref[0, 0] = acc.astype(o_ref.dtype)
            continue
        for c in range(dil):
            cols = [acc_ref[s, pl.ds(c, tm // dil, stride=dil), :] for s in range(n_slab)]
            o_ref[0, c] = jnp.concatenate(cols, axis=1).astype(o_ref.dtype)


def qkv_projection(x, w, bsz, seq, dils, tm, tn, scaled_blocks, scale):
    m, k = x.shape
    n = w.shape[1]
    tpb = seq // tm
    out_spec = lambda dil: pl.BlockSpec((1, dil, tm // dil, tn),
                                        lambda j, i: (i // tpb, 0, i % tpb, j))
    return pl.pallas_call(
        functools.partial(_qkv_body, dils=dils, scaled_blocks=scaled_blocks, scale=scale),
        grid=(n // tn, m // tm),
        in_specs=[pl.BlockSpec((tm, k), lambda j, i: (i, 0)),
                  pl.BlockSpec((k, tn), lambda j, i: (0, j))],
        out_specs=[out_spec(dil) for dil in dils],
        out_shape=[jax.ShapeDtypeStruct((bsz, dil, seq // dil, n), BF16) for dil in dils],
        scratch_shapes=[pltpu.VMEM((tn // LANES, tm, LANES), F32)],
        compiler_params=_cparams("arbitrary", "arbitrary"),
        name="qkv_projection",
    )(x, w)


def _dsa_pro_body(x_ref, win_ref, qn_ref, kvn_ref, wuq_ref, wuqi_ref, wuk_ref,
                  qlat_ref, qidx_ref, widx_ref, ckv_ref, kidx_ref):
    proj = jnp.dot(x_ref[...], win_ref[...], preferred_element_type=F32)
    c_q = _rms_norm(proj[:, :Q_LORA], qn_ref[...]).astype(BF16)
    c_kv = _rms_norm(proj[:, Q_LORA:Q_LORA + KV_LORA], kvn_ref[...])
    ckv_ref[...] = c_kv.astype(BF16)
    kidx_ref[...] = proj[:, 768:768 + IDX_DIM].astype(BF16)
    widx_ref[...] = (proj[:, 896:1024] * (IDX_HEADS ** -0.5)).T[:IDX_HEADS]
    q = jnp.dot(c_q, wuq_ref[...], preferred_element_type=F32)
    for h in range(N_HEADS):
        qh = q[:, h * HEAD_DIM:(h + 1) * HEAD_DIM].astype(BF16)
        ql = jnp.dot(qh, wuk_ref[h], preferred_element_type=F32) * (HEAD_DIM ** -0.5)
        qlat_ref[h] = ql.astype(BF16)
    qi = jnp.dot(c_q, wuqi_ref[...], preferred_element_type=F32) * (IDX_DIM ** -0.5)
    for h in range(IDX_HEADS):
        qidx_ref[h] = qi[:, h * IDX_DIM:(h + 1) * IDX_DIM].astype(BF16)


def dsa_prologue(xb, w_in_p, q_norm, kv_norm, w_uq, w_uq_idx, w_uk, tm):
    n, d = xb.shape
    hw = N_HEADS * HEAD_DIM
    return pl.pallas_call(
        _dsa_pro_body,
        grid=(n // tm,),
        in_specs=[pl.BlockSpec((tm, d), lambda i: (i, 0)),
                  _resident((d, 1024), lambda i: (0, 0)),
                  _resident((1, Q_LORA), lambda i: (0, 0)),
                  _resident((1, KV_LORA), lambda i: (0, 0)),
                  _resident((Q_LORA, hw), lambda i: (0, 0)),
                  _resident((Q_LORA, IDX_HEADS * IDX_DIM), lambda i: (0, 0)),
                  _resident((N_HEADS, HEAD_DIM, KV_LORA), lambda i: (0, 0, 0))],
        out_specs=[pl.BlockSpec((N_HEADS, tm, KV_LORA), lambda i: (0, i, 0)),
                   pl.BlockSpec((IDX_HEADS, tm, IDX_DIM), lambda i: (0, i, 0)),
                   pl.BlockSpec((IDX_HEADS, tm), lambda i: (0, i)),
                   pl.BlockSpec((tm, KV_LORA), lambda i: (i, 0)),
                   pl.BlockSpec((tm, IDX_DIM), lambda i: (i, 0))],
        out_shape=[jax.ShapeDtypeStruct((N_HEADS, n, KV_LORA), BF16),
                   jax.ShapeDtypeStruct((IDX_HEADS, n, IDX_DIM), BF16),
                   jax.ShapeDtypeStruct((IDX_HEADS, n), F32),
                   jax.ShapeDtypeStruct((n, KV_LORA), BF16),
                   jax.ShapeDtypeStruct((n, IDX_DIM), BF16)],
        compiler_params=_cparams("arbitrary"),
        name="dsa_prologue",
    )(xb, w_in_p, q_norm, kv_norm, w_uq, w_uq_idx, w_uk)


DSA_TQ = 128
DSA_TK = 256


def _dsa_attn_body(qlat_ref, qidx_ref, widxt_ref, ckv_ref, kidx_ref, tab_ref, wuv_ref, o_ref,
                   keys_ref, keyst_ref, acc_ref, m_ref, pstar_ref,
                   s0_ref, s1_ref, p0_ref, p1_ref, a0_ref, a1_ref,
                   *, seq, topk, n_tab):
    tq, tk = DSA_TQ, DSA_TK
    i = pl.program_id(1)
    q0 = i * tq
    nk = (q0 + tq + tk - 1) // tk
    qpos = q0 + lax.broadcasted_iota(I32, (tq, tk), 0)
    col = lax.broadcasted_iota(I32, (tq, tk), 1)
    qpos_t = q0 + lax.broadcasted_iota(I32, (tk, tq), 1)
    krow_t = lax.broadcasted_iota(I32, (tk, tq), 0)

    def dup(v):
        return jnp.concatenate([v] * (tk // LANES), axis=1)

    def score_tile(j, carry):
        k0 = pl.multiple_of(j * tk, tk)
        kt = kidx_ref[pl.ds(k0, tk), :]
        rel = lax.dot_general(kt, qidx_ref[...].reshape(IDX_HEADS * tq, IDX_DIM),
                              (((1,), (1,)), ((), ())), preferred_element_type=F32)
        s = jnp.zeros((tk, tq), F32)
        for h in range(IDX_HEADS):
            s = s + jnp.maximum(rel[:, h * tq:(h + 1) * tq], 0.0) * widxt_ref[h:h + 1, :]
        bits = pltpu.bitcast(s, I32)
        key = bits ^ ((bits >> 31) & 0x7FFFFFFF)
        key = jnp.where(k0 + krow_t <= qpos_t, key, INT_MIN)
        keyst_ref[j] = key
        keys_ref[j] = key.T
        return carry

    lax.fori_loop(0, nk, score_tile, 0)

    @pl.when(nk % 2 == 1)
    def _():
        keyst_ref[nk] = jnp.full((tk, tq), INT_MIN, I32)

    def count(pred):
        def body(jj, acc):
            for j in (2 * jj, 2 * jj + 1):
                ind = jnp.where(pred(keyst_ref[j], j), 1.0, 0.0)
                parts = [ind[r * 8:(r + 1) * 8, :] for r in range(tk // 8)]
                while len(parts) > 1:
                    parts = [a + b for a, b in zip(parts[::2], parts[1::2])]
                acc = acc + parts[0]
            return acc
        acc = lax.fori_loop(0, (nk + 1) // 2, body, jnp.zeros((8, tq), F32))
        return jnp.sum(acc, axis=0, keepdims=True)

    def bit_step(t, carry):
        v, cge = carry
        cand_u = v | jnp.left_shift(jnp.int32(1), 31 - t)
        cand = cand_u ^ INT_MIN
        cnt = count(lambda key, j: key >= cand)
        take = cnt >= topk
        return jnp.where(take, cand_u, v), jnp.where(take, cnt, cge)

    v, cge = lax.fori_loop(0, 32, bit_step,
                           (jnp.zeros((1, tq), I32), jnp.zeros((1, tq), F32)))
    thr_t = v ^ INT_MIN

    pstar_ref[...] = jnp.full((1, tq), seq, I32)
    surplus = jnp.where((v != 0) & (cge > topk), 1.0, 0.0)

    @pl.when(jnp.max(surplus) > 0.0)
    def _():
        cgt = count(lambda key, j: key > thr_t)
        need = topk - cgt

        def pos_step(t, p):
            cand = p | jnp.left_shift(jnp.int32(1), (seq.bit_length() - 1) - t)
            before = count(lambda key, j: (key == thr_t) & (j * tk + krow_t < cand))
            return jnp.where(before < need, cand, p)

        p = lax.fori_loop(0, seq.bit_length(), pos_step, jnp.zeros((1, tq), I32))
        pstar_ref[...] = jnp.where(surplus > 0.0, p, seq)

    def per_row(v_t):
        return dup(jnp.broadcast_to(v_t, (LANES, tq)).T)

    thr_w = per_row(thr_t)
    pstar_w = per_row(pstar_ref[...])

    acc_ref[...] = jnp.zeros_like(acc_ref)
    m_ref[...] = jnp.full_like(m_ref, NEG)
    dq = q0 // LANES

    n_tiles = seq // tk
    qlat2d = lambda: qlat_ref[...].reshape(N_HEADS * tq, KV_LORA)

    def kv_tile(j):
        return ckv_ref[pl.ds(pl.multiple_of(jnp.clip(j, 0, n_tiles - 1) * tk, tk), tk), :]

    def logits(j):
        return lax.dot_general(qlat2d(), kv_tile(j), (((1,), (1,)), ((), ())),
                               preferred_element_type=F32)

    s_bufs, p_bufs, alpha_bufs = (s0_ref, s1_ref), (p0_ref, p1_ref), (a0_ref, a1_ref)
    s0_ref[...] = logits(0)
    p1_ref[...] = jnp.zeros(p1_ref.shape, BF16)
    a1_ref[...] = jnp.ones(a1_ref.shape, F32)

    def attn_step(j, cur):
        prv = 1 - cur
        s_ref, p_ref, alpha_ref = s_bufs[cur], p_bufs[cur], alpha_bufs[cur]
        s_bufs[prv][...] = logits(j + 1)

        kpos = j * tk + col
        key = keys_ref[jnp.minimum(j, nk - 1)]
        sel = (kpos <= qpos) & ((key > thr_w) | ((key == thr_w) & (kpos <= pstar_w)))
        maskadd = jnp.where(sel, 0.0, NEG)
        dbase = dq - j * (tk // LANES) + 1
        didx = [jnp.clip(dbase - c, 0, n_tab - 1) for c in range(tk // LANES)]
        for h in range(N_HEADS):
            rows = slice(h * tq, (h + 1) * tq)
            bias = jnp.concatenate([tab_ref[d, h] for d in didx], axis=1)
            s = s_ref[rows, :] + bias + maskadd
            s_ref[rows, :] = s
            m_old = m_ref[rows, :]
            m_new = jnp.maximum(m_old, jnp.max(s, axis=1, keepdims=True))
            alpha_ref[rows, :] = jnp.exp(m_old - m_new)
            m_ref[rows, :] = m_new

        kv_ext = jnp.concatenate([kv_tile(j - 1), jnp.ones((tk, LANES), BF16)], axis=1)
        pv = jnp.dot(p_bufs[prv][...], kv_ext, preferred_element_type=F32)
        acc_ref[...] = alpha_bufs[prv][...] * acc_ref[...] + pv

        for h in range(N_HEADS):
            rows = slice(h * tq, (h + 1) * tq)
            p_ref[rows, :] = jnp.exp(s_ref[rows, :] - m_ref[rows, :]).astype(BF16)

    def attn_pair(jj, carry):
        attn_step(2 * jj, 0)
        attn_step(2 * jj + 1, 1)
        return carry

    n_pairs = (nk + 1) // 2
    lax.fori_loop(0, n_pairs, attn_pair, 0)
    kv_ext = jnp.concatenate([kv_tile(2 * n_pairs - 1), jnp.ones((tk, LANES), BF16)], axis=1)
    acc_ref[...] = a1_ref[...] * acc_ref[...] + jnp.dot(p1_ref[...], kv_ext,
                                                        preferred_element_type=F32)

    outs = []
    for h in range(N_HEADS):
        rows = slice(h * tq, (h + 1) * tq)
        inv_l = 1.0 / acc_ref[rows, KV_LORA:]
        o_lat = acc_ref[rows, :KV_LORA] * jnp.concatenate([inv_l] * (KV_LORA // LANES), axis=1)
        outs.append(jnp.dot(o_lat.astype(BF16), wuv_ref[h], preferred_element_type=F32))
    o_ref[...] = jnp.concatenate(outs, axis=1).astype(o_ref.dtype)


def dsa_attention(qlat, qidx, widx, ckv, kidx, tab, w_uv, bsz, seq):
    tq, tk = DSA_TQ, DSA_TK
    nq = seq // tq
    n = bsz * seq
    n_tab = tab.shape[0]
    topk = min(IDX_TOPK_MAX, seq // 4)
    body = functools.partial(_dsa_attn_body, seq=seq, topk=topk, n_tab=n_tab)
    return pl.pallas_call(
        body,
        grid=(bsz, nq),
        in_specs=[pl.BlockSpec((N_HEADS, tq, KV_LORA), lambda b, i: (0, b * nq + i, 0)),
                  pl.BlockSpec((IDX_HEADS, tq, IDX_DIM), lambda b, i: (0, b * nq + i, 0)),
                  pl.BlockSpec((IDX_HEADS, tq), lambda b, i: (0, b * nq + i)),
                  pl.BlockSpec((seq, KV_LORA), lambda b, i: (b, 0)),
                  pl.BlockSpec((seq, IDX_DIM), lambda b, i: (b, 0)),
                  _resident(tab.shape, lambda b, i: (0, 0, 0, 0)),
                  _resident((N_HEADS, KV_LORA, HEAD_DIM), lambda b, i: (0, 0, 0))],
        out_specs=pl.BlockSpec((tq, N_HEADS * HEAD_DIM), lambda b, i: (b * nq + i, 0)),
        out_shape=jax.ShapeDtypeStruct((n, N_HEADS * HEAD_DIM), BF16),
        scratch_shapes=[pltpu.VMEM((seq // tk, tq, tk), I32),
                        pltpu.VMEM((seq // tk, tk, tq), I32),
                        pltpu.VMEM((N_HEADS * tq, KV_LORA + LANES), F32),
                        pltpu.VMEM((N_HEADS * tq, 1), F32),
                        pltpu.VMEM((1, tq), I32),
                        pltpu.VMEM((N_HEADS * tq, tk), F32),
                        pltpu.VMEM((N_HEADS * tq, tk), F32),
                        pltpu.VMEM((N_HEADS * tq, tk), BF16),
                        pltpu.VMEM((N_HEADS * tq, tk), BF16),
                        pltpu.VMEM((N_HEADS * tq, 1), F32),
                        pltpu.VMEM((N_HEADS * tq, 1), F32)],
        compiler_params=_cparams("arbitrary", "arbitrary"),
        name="dsa_attention",
    )(qlat, qidx, widx, ckv, kidx, tab, w_uv)


def dsa_bias_buckets(seq):
    far = np.maximum(np.arange(seq), N_BUCKETS // 2).astype(np.float64)
    far_bucket = np.floor(np.log(far / (N_BUCKETS // 2)) / math.log(MAX_DISTANCE / (N_BUCKETS // 2))
                          * (N_BUCKETS - N_BUCKETS // 2))
    unsat = np.nonzero(far_bucket < N_BUCKETS - 1 - N_BUCKETS // 2)[0]
    n_sat = int(unsat[-1]) + 1 if unsat.size else 0
    n_tab = min(seq // LANES + 1, -(-(n_sat + LANES - 1) // LANES) + 2)
    t = np.arange(n_tab)[:, None, None]
    qi = np.arange(LANES)[None, :, None]
    ki = np.arange(LANES)[None, None, :]
    dist = np.maximum(LANES * (t - 1) + qi - ki, 0)
    return _t5_bucket(jnp.asarray(dist, I32))


def _dil_body(q_ref, kp_ref, km_ref, vp_ref, vm_ref, bias_ref, o_ref, lse_ref, *, look):
    nblk = pl.program_id(2)
    tq = q_ref.shape[2]
    ncol = look + tq
    colv = lax.broadcasted_iota(I32, (1, ncol), 1)
    first = jnp.where((nblk == 0) & (colv < look), NEG, 0.0)
    lse_ref[0, 0] = jnp.zeros(lse_ref.shape[2:], F32)
    ones = jnp.ones((ncol, HEAD_DIM), BF16)
    outs = []
    for h in range(N_HEADS):
        sl = slice(h * HEAD_DIM, (h + 1) * HEAD_DIM)
        q = q_ref[0, 0, :, sl]
        k = jnp.concatenate([kp_ref[0, 0, :, sl], km_ref[0, 0, :, sl]], axis=0)
        v = jnp.concatenate([vp_ref[0, 0, :, sl], vm_ref[0, 0, :, sl]], axis=0)
        s = lax.dot_general(q, k, (((1,), (1,)), ((), ())), preferred_element_type=F32)
        s = s + bias_ref[0, h] + first
        m = jnp.max(s, axis=1, keepdims=True)
        p = jnp.exp(s - m)
        ov = jnp.dot(p.astype(BF16), jnp.concatenate([v, ones], axis=1),
                     preferred_element_type=F32)
        l = ov[:, HEAD_DIM:]
        outs.append(ov[:, :HEAD_DIM] * (1.0 / l))
        lse_ref[0, 0, :, h:h + 1] = m + jnp.log(l[:, :1])
    o_ref[0, 0] = jnp.concatenate(outs, axis=1).astype(o_ref.dtype)


def dilated_branch(qkv, bias, look):
    bsz, dil, ls, _ = qkv.shape
    hw = N_HEADS * HEAD_DIM
    tq = bias.shape[2]
    nb = ls // tq
    r = tq // look
    prev = lambda part: (lambda b, c, n: (b, c, jnp.maximum(n * r - 1, 0), part))
    main = lambda part: (lambda b, c, n: (b, c, n, part))
    return pl.pallas_call(
        functools.partial(_dil_body, look=look),
        grid=(bsz, dil, nb),
        in_specs=[pl.BlockSpec((1, 1, tq, hw), main(0)),
                  pl.BlockSpec((1, 1, look, hw), prev(1)),
                  pl.BlockSpec((1, 1, tq, hw), main(1)),
                  pl.BlockSpec((1, 1, look, hw), prev(2)),
                  pl.BlockSpec((1, 1, tq, hw), main(2)),
                  _resident(bias.shape, lambda b, c, n: (0, 0, 0, 0))],
        out_specs=[pl.BlockSpec((1, 1, tq, hw), lambda b, c, n: (b, c, n, 0)),
                   pl.BlockSpec((1, 1, tq, LANES), lambda b, c, n: (b, c, n, 0))],
        out_shape=[jax.ShapeDtypeStruct((bsz, dil, ls, hw), BF16),
                   jax.ShapeDtypeStruct((bsz, dil, ls, LANES), F32)],
        compiler_params=_cparams("arbitrary", "arbitrary", "arbitrary"),
        name=f"dilated_d{dil}",
    )(qkv, qkv, qkv, qkv, qkv, bias)


def dilated_bias_buckets(tq, look, dil):
    rel = np.arange(tq)[:, None] + look - np.arange(look + tq)[None, :]
    band = (rel >= 0) & (rel <= look)
    bucket = _t5_bucket(jnp.asarray(dil * np.maximum(rel, 0), I32))
    return jnp.where(jnp.asarray(band), bucket, MASK_BUCKET)[None]


def _post_attn_body(*refs, dils, alpha):
    n_parts = len(dils)
    merge = n_parts > 1
    o_refs = refs[:n_parts]
    lse_refs = refs[n_parts:2 * n_parts] if merge else ()
    base = n_parts + len(lse_refs)
    wo_ref, x_ref, g_ref, b_ref, xo_ref, xb_ref = refs[base:base + 6]
    scratch = list(refs[base + 6:])
    tm = x_ref.shape[0]
    head = lambda h: slice(h * HEAD_DIM, (h + 1) * HEAD_DIM)
    if not merge:
        o = o_refs[0][0, 0]
    else:
        lses, o_heads = [], []
        for part, dil in enumerate(dils):
            if dil == 1:
                lses.append(lse_refs[part][0, 0])
                o_heads.append(lambda h, r=o_refs[part]: r[0, 0, :, head(h)].astype(F32))
                continue
            o_scr, l_scr = scratch.pop(0), scratch.pop(0)
            rows = tm // dil
            for c in range(dil):
                l_scr[pl.ds(c, rows, stride=dil), :] = lse_refs[part][0, c]
                for h in range(N_HEADS):
                    o_scr[h, pl.ds(c, rows, stride=dil), :] = (
                        o_refs[part][0, c, :, head(h)].astype(F32))
            lses.append(l_scr[...])
            o_heads.append(lambda h, r=o_scr: r[h])
        mx = functools.reduce(jnp.maximum, lses)
        ws = [jnp.exp(v - mx) for v in lses]
        inv = 1.0 / functools.reduce(lambda a, b: a + b, ws)
        ws = [w * inv for w in ws]
        cols = []
        for h in range(N_HEADS):
            acc = None
            for part in range(n_parts):
                term = ws[part][:, h:h + 1] * o_heads[part](h)
                acc = term if acc is None else acc + term
            cols.append(acc)
        o = jnp.concatenate(cols, axis=1).astype(BF16)
    hproj = jnp.dot(o, wo_ref[...], preferred_element_type=F32)
    y = _layer_norm(alpha * x_ref[...] + hproj, g_ref[...], b_ref[...])
    xo_ref[...] = y
    xb_ref[...] = y.astype(BF16)


def post_attention(o_parts, lse_parts, w_o, x, g, b, alpha, tm):
    n, d = x.shape
    hw = N_HEADS * HEAD_DIM
    dils = tuple(o.shape[1] for o in o_parts)
    seq = o_parts[0].shape[1] * o_parts[0].shape[2]
    tpb = seq // tm
    row = lambda width: pl.BlockSpec((tm, width), lambda i: (i, 0))
    by_class = lambda dil, width: pl.BlockSpec((1, dil, tm // dil, width),
                                               lambda i: (i // tpb, 0, i % tpb, 0))
    in_specs = ([by_class(dil, hw) for dil in dils]
                + [by_class(dil, LANES) for dil in dils[:len(lse_parts)]]
                + [_resident((hw, d), lambda i: (0, 0)), row(d),
                   _resident((1, d), lambda i: (0, 0)), _resident((1, d), lambda i: (0, 0))])
    scratch = []
    for dil in dils:
        if dil > 1:
            scratch += [pltpu.VMEM((N_HEADS, tm, HEAD_DIM), F32), pltpu.VMEM((tm, LANES), F32)]
    return pl.pallas_call(
        functools.partial(_post_attn_body, dils=dils, alpha=alpha),
        grid=(n // tm,),
        in_specs=in_specs,
        out_specs=[row(d), row(d)],
        out_shape=[jax.ShapeDtypeStruct((n, d), F32), jax.ShapeDtypeStruct((n, d), BF16)],
        scratch_shapes=scratch,
        compiler_params=_cparams("arbitrary"),
        name=f"post_attention_{len(dils)}",
    )(*o_parts, *lse_parts, w_o, x, g, b)


def _router_body(x_ref, rwt_ref, rb_ref, route_ref, gates_ref, counts_ref, carry_ref):
    tm = x_ref.shape[0]

    @pl.when(pl.program_id(0) == 0)
    def _():
        carry_ref[...] = jnp.zeros_like(carry_ref)

    logits = lax.dot_general(rwt_ref[...], x_ref[...], (((1,), (1,)), ((), ())),
                             preferred_element_type=F32,
                             precision=lax.Precision.HIGHEST)
    rows = [logits[e:e + 1, :] for e in range(N_EXPERTS)]
    mx = functools.reduce(jnp.maximum, rows)
    ex = [jnp.exp(r - mx) for r in rows]
    inv = 1.0 / functools.reduce(lambda a, b: a + b, ex)
    probs = [e * inv for e in ex]
    sel = [probs[e] + rb_ref[e] for e in range(N_EXPERTS)]

    zero_i = jnp.zeros((1, tm), I32)
    best_t = None
    for g in range(N_GROUPS):
        vals = sel[g * EXPERTS_PER_GROUP:(g + 1) * EXPERTS_PER_GROUP]
        prb = probs[g * EXPERTS_PER_GROUP:(g + 1) * EXPERTS_PER_GROUP]
        top, top_i, top_p = vals[0], zero_i, prb[0]
        for j in range(1, EXPERTS_PER_GROUP):
            c = vals[j] > top
            top = jnp.where(c, vals[j], top)
            top_i = jnp.where(c, j, top_i)
            top_p = jnp.where(c, prb[j], top_p)
        sec = jnp.full((1, tm), -jnp.inf, F32)
        sec_i, sec_p = zero_i, jnp.zeros((1, tm), F32)
        for j in range(EXPERTS_PER_GROUP):
            c = (vals[j] > sec) & (top_i != j)
            sec = jnp.where(c, vals[j], sec)
            sec_i = jnp.where(c, j, sec_i)
            sec_p = jnp.where(c, prb[j], sec_p)
        tot = top + sec
        if best_t is None:
            best_t = tot
            e0, e1 = top_i + g * EXPERTS_PER_GROUP, sec_i + g * EXPERTS_PER_GROUP
            p0, p1 = top_p, sec_p
        else:
            c = tot > best_t
            best_t = jnp.where(c, tot, best_t)
            e0 = jnp.where(c, top_i + g * EXPERTS_PER_GROUP, e0)
            e1 = jnp.where(c, sec_i + g * EXPERTS_PER_GROUP, e1)
            p0 = jnp.where(c, top_p, p0)
            p1 = jnp.where(c, sec_p, p1)
    ginv = 1.0 / (p0 + p1)
    g0, g1 = p0 * ginv, p1 * ginv
    gates_ref[...] = jnp.concatenate([g0, g1, jnp.zeros((LANES - 2, tm), F32)], axis=0).T

    assign = jnp.concatenate([jnp.where((e0 == e) | (e1 == e), 1.0, 0.0)
                              for e in range(N_EXPERTS)], axis=0).astype(BF16)
    upper = jnp.where(lax.broadcasted_iota(I32, (tm, tm), 0)
                      <= lax.broadcasted_iota(I32, (tm, tm), 1), 1.0, 0.0).astype(BF16)
    prefix = jnp.dot(assign, upper, preferred_element_type=F32) + carry_ref[...]
    rank0 = jnp.zeros((1, tm), F32)
    rank1 = jnp.zeros((1, tm), F32)
    for e in range(N_EXPERTS):
        rank0 = jnp.where(e0 == e, prefix[e:e + 1, :], rank0)
        rank1 = jnp.where(e1 == e, prefix[e:e + 1, :], rank1)
    total = prefix[:, tm - 1:tm]
    carry_ref[...] = total
    counts_ref[...] = jnp.broadcast_to(total, counts_ref.shape).astype(I32)
    route_ref[...] = jnp.concatenate(
        [e0, e1, (rank0 - 1.0).astype(I32), (rank1 - 1.0).astype(I32), jnp.zeros((4, tm), I32)],
        axis=0)


def router(x, router_wt, router_b, tm):
    n, d = x.shape
    return pl.pallas_call(
        _router_body,
        grid=(n // tm,),
        in_specs=[pl.BlockSpec((tm, d), lambda i: (i, 0)),
                  _resident((N_EXPERTS, d), lambda i: (0, 0)),
                  pl.BlockSpec(memory_space=pltpu.SMEM)],
        out_specs=[pl.BlockSpec((8, tm), lambda i: (0, i)),
                   pl.BlockSpec((tm, LANES), lambda i: (i, 0)),
                   pl.BlockSpec((N_EXPERTS, LANES), lambda i: (0, 0))],
        out_shape=[jax.ShapeDtypeStruct((8, n), I32),
                   jax.ShapeDtypeStruct((n, LANES), F32),
                   jax.ShapeDtypeStruct((N_EXPERTS, LANES), I32)],
        scratch_shapes=[pltpu.VMEM((N_EXPERTS, 1), F32)],
        compiler_params=_cparams("arbitrary"),
        name="router",
    )(x, router_wt, router_b)


EXPERT_TILE = 512
COMBINE_TILE = 256


def _row_copy(src_hbm, row, dst, dst_row, sem):
    return pltpu.make_async_copy(src_hbm.at[pl.ds(row, 1), :], dst.at[pl.ds(dst_row, 1), :], sem)


def _experts_body(te_ref, nu_ref, src_next_ref, src_first_ref, x_hbm, wg_ref, wu_ref, wd_ref,
                  ys_ref, xbuf, sem, wgb, wub, wdb):
    i = pl.program_id(0)
    n_used = nu_ref[0]
    slot = i % 2
    tmx = ys_ref.shape[0]

    def issue(src_ref, dst_slot):
        for r in range(tmx):
            _row_copy(x_hbm, src_ref[0, 0, r], xbuf.at[dst_slot], r, sem.at[dst_slot]).start()

    @pl.when(i == 0)
    def _():
        issue(src_first_ref, 0)

    @pl.when(i + 1 < n_used)
    def _():
        issue(src_next_ref, 1 - slot)

    @pl.when(i < n_used)
    def _():
        def wait_row(r, c):
            _row_copy(x_hbm, 0, xbuf.at[slot], r, sem.at[slot]).wait()
            return c
        lax.fori_loop(0, tmx, wait_row, 0, unroll=8)

        @pl.when((i == 0) | (te_ref[i] != te_ref[jnp.maximum(i - 1, 0)]))
        def _():
            wgb[...] = wg_ref[0, 0].astype(BF16)
            wub[...] = wu_ref[0, 0].astype(BF16)
            wdb[...] = wd_ref[0, 0].astype(BF16)

        xb = xbuf[slot].astype(BF16)
        gate = jnp.dot(xb, wgb[...], preferred_element_type=F32)
        up = jnp.dot(xb, wub[...], preferred_element_type=F32)
        hidden = (gate * jax.nn.sigmoid(gate)) * up
        ys_ref[...] = jnp.dot(hidden.astype(BF16), wdb[...], preferred_element_type=F32)

    @pl.when(i >= n_used)
    def _():
        ys_ref[...] = jnp.zeros_like(ys_ref)


def moe_experts(x, src_tiles, tile_expert, n_used, w_gate, w_up, w_down, layer):
    n, d = x.shape
    n_tiles = src_tiles.shape[0]
    tmx = EXPERT_TILE
    smem_tile = lambda index_map: pl.BlockSpec((1, 1, tmx), index_map, memory_space=pltpu.SMEM)
    by_expert = lambda shape: pl.BlockSpec((1,) + shape, lambda i, te, nu: (layer, te[i], 0, 0))
    grid_spec = pltpu.PrefetchScalarGridSpec(
        num_scalar_prefetch=2,
        grid=(n_tiles,),
        in_specs=[smem_tile(lambda i, te, nu: (jnp.minimum(i + 1, n_tiles - 1), 0, 0)),
                  smem_tile(lambda i, te, nu: (0, 0, 0)),
                  pl.BlockSpec(memory_space=pl.ANY),
                  by_expert((1, d, D_EXPERT)), by_expert((1, d, D_EXPERT)),
                  by_expert((1, D_EXPERT, d))],
        out_specs=pl.BlockSpec((tmx, d), lambda i, te, nu: (i, 0)),
        scratch_shapes=[pltpu.VMEM((2, tmx, d), F32),
                        pltpu.SemaphoreType.DMA((2,)),
                        pltpu.VMEM((d, D_EXPERT), BF16),
                        pltpu.VMEM((d, D_EXPERT), BF16),
                        pltpu.VMEM((D_EXPERT, d), BF16)])
    return pl.pallas_call(
        _experts_body,
        grid_spec=grid_spec,
        out_shape=jax.ShapeDtypeStruct((n_tiles * tmx, d), F32),
        compiler_params=_cparams("arbitrary"),
        name="moe_experts",
    )(tile_expert, n_used, src_tiles, src_tiles, x, w_gate, w_up, w_down)


def _combine_body(pos_next_ref, pos_first_ref, ys_hbm, gates_ref, x_ref, g_ref, b_ref,
                  xo_ref, xbo_ref, ybuf, sem, *, alpha):
    i = pl.program_id(0)
    n_steps = pl.num_programs(0)
    slot = i % 2
    tm = x_ref.shape[0]

    def issue(pos_ref, dst_slot):
        for k in range(2):
            for r in range(tm):
                _row_copy(ys_hbm, pos_ref[0, k, r], ybuf.at[dst_slot, k], r,
                          sem.at[dst_slot]).start()

    @pl.when(i == 0)
    def _():
        issue(pos_first_ref, 0)

    @pl.when(i + 1 < n_steps)
    def _():
        issue(pos_next_ref, 1 - slot)

    def wait_row(r, c):
        for k in range(2):
            _row_copy(ys_hbm, 0, ybuf.at[slot, k], r, sem.at[slot]).wait()
        return c
    lax.fori_loop(0, tm, wait_row, 0, unroll=8)

    gates = gates_ref[...]
    y = gates[:, 0:1] * ybuf[slot, 0] + gates[:, 1:2] * ybuf[slot, 1]
    out = _layer_norm(alpha * x_ref[...] + y, g_ref[...], b_ref[...])
    xo_ref[...] = out
    xbo_ref[...] = out.astype(BF16)


def moe_combine(ys, pos_tiles, gates, x, g, b, alpha):
    n, d = x.shape
    tm = COMBINE_TILE
    n_steps = n // tm
    smem_tile = lambda index_map: pl.BlockSpec((1, 2, tm), index_map, memory_space=pltpu.SMEM)
    row = lambda width: pl.BlockSpec((tm, width), lambda i: (i, 0))
    return pl.pallas_call(
        functools.partial(_combine_body, alpha=alpha),
        grid=(n_steps,),
        in_specs=[smem_tile(lambda i: (jnp.minimum(i + 1, n_steps - 1), 0, 0)),
                  smem_tile(lambda i: (0, 0, 0)),
                  pl.BlockSpec(memory_space=pl.ANY),
                  row(LANES), row(d),
                  _resident((1, d), lambda i: (0, 0)), _resident((1, d), lambda i: (0, 0))],
        out_specs=[row(d), row(d)],
        out_shape=[jax.ShapeDtypeStruct((n, d), F32), jax.ShapeDtypeStruct((n, d), BF16)],
        scratch_shapes=[pltpu.VMEM((2, 2, tm, d), F32), pltpu.SemaphoreType.DMA((2,))],
        compiler_params=_cparams("arbitrary"),
        name="moe_combine",
    )(pos_tiles, pos_tiles, ys, gates, x, g, b)


def moe_layer(x, router_wt, router_b, w_gate, w_up, w_down, layer, g, b, alpha):
    n, d = x.shape
    tmx = EXPERT_TILE
    n_tiles = (2 * n) // tmx + N_EXPERTS
    route, gates, counts = router(x, router_wt, router_b, tm=512)
    counts = counts[:, 0]
    tiles_e = (counts + tmx - 1) // tmx
    tile_end = jnp.cumsum(tiles_e)
    row_off = (tile_end - tiles_e) * tmx
    pos0 = row_off[route[0]] + route[2]
    pos1 = row_off[route[1]] + route[3]
    tok = jnp.arange(n, dtype=I32)
    src = jnp.zeros((n_tiles * tmx,), I32).at[jnp.concatenate([pos0, pos1])].set(
        jnp.concatenate([tok, tok]), unique_indices=True)
    tile_expert = jnp.minimum(
        jnp.sum(jnp.arange(n_tiles, dtype=I32)[:, None] >= tile_end[None, :], axis=1),
        N_EXPERTS - 1).astype(I32)
    n_used = tile_end[-1:].astype(I32)
    ys = moe_experts(x, src.reshape(n_tiles, 1, tmx), tile_expert, n_used, w_gate, w_up, w_down,
                     layer)
    tm = COMBINE_TILE
    pos_tiles = jnp.stack([pos0, pos1]).reshape(2, n // tm, tm).transpose(1, 0, 2)
    return moe_combine(ys, pos_tiles, gates, x, g, b, alpha)


def kernel(x, rel_bias, router_w, router_b, a_w_in, a_q_norm, a_kv_norm, a_w_uq, a_w_uq_idx,
           a_w_uk, a_w_uv, a_w_o, b_w_in, b_w_o, ln1_g, ln1_b, ln2_g, ln2_b,
           moe_w_gate, moe_w_up, moe_w_down):
    bsz, seq, d = x.shape
    depth = ln1_g.shape[0]
    n = bsz * seq
    hw = N_HEADS * HEAD_DIM
    alpha = (2 * depth) ** 0.25

    x = x.reshape(n, d)
    xb = x.astype(BF16)
    router_wt = router_w.T

    dsa_tab = bias_expand(dsa_bias_buckets(seq), rel_bias)
    dil_bias = []
    for window, dil in DILATED_PATTERNS:
        look = window // dil
        tq = min(2 * look, seq // dil)
        dil_bias.append(bias_expand(dilated_bias_buckets(tq, look, dil), rel_bias))

    for i in range(depth):
        j = i // 2
        if i % 2 == 0:
            w_in = a_w_in[j]
            kv_end = Q_LORA + KV_LORA
            w_in_p = jnp.concatenate(
                [w_in[:, :kv_end + IDX_DIM], jnp.zeros((d, LANES - IDX_DIM), F32),
                 w_in[:, kv_end + IDX_DIM:], jnp.zeros((d, LANES - IDX_HEADS), F32)],
                axis=1).astype(BF16)
            qlat, qidx, widx, ckv, kidx = dsa_prologue(
                xb, w_in_p, a_q_norm[j][None], a_kv_norm[j][None], a_w_uq[j].astype(BF16),
                a_w_uq_idx[j].astype(BF16), a_w_uk[j].astype(BF16), tm=256)
            o = dsa_attention(qlat, qidx, widx, ckv, kidx, dsa_tab, a_w_uv[j].astype(BF16),
                              bsz, seq)
            o_parts, lse_parts, w_o = [o.reshape(bsz, 1, seq, hw)], [], a_w_o[j]
        else:
            dils = tuple(dil for _, dil in DILATED_PATTERNS)
            qkvs = qkv_projection(xb, b_w_in[j].astype(BF16), bsz, seq, dils, tm=512, tn=1024,
                                  scaled_blocks=hw // 1024, scale=HEAD_DIM ** -0.5)
            o_parts, lse_parts = [], []
            for (window, dil), qkv, bias in zip(DILATED_PATTERNS, qkvs, dil_bias):
                o_i, lse_i = dilated_branch(qkv, bias, window // dil)
                o_parts.append(o_i)
                lse_parts.append(lse_i)
            w_o = b_w_o[j]
        x, xb = post_attention(o_parts, lse_parts, w_o.astype(BF16), x, ln1_g[i][None],
                               ln1_b[i][None], alpha, tm=256)
        x, xb = moe_layer(x, router_wt, router_b, moe_w_gate, moe_w_up, moe_w_down, i,
                          ln2_g[i][None], ln2_b[i][None], alpha)
    return x.reshape(bsz, seq, d)
```

```python
import functools
import math

import numpy as np
import jax
import jax.numpy as jnp
from jax import lax
from jax.experimental import pallas as pl
from jax.experimental.pallas import tpu as pltpu

F32 = jnp.float32
BF16 = jnp.bfloat16
I32 = jnp.int32

N_HEADS = 16
HEAD_DIM = 128
Q_LORA = 512
KV_LORA = 256
IDX_HEADS = 16
IDX_DIM = 64
IDX_TOPK_MAX = 256
DILATED_PATTERNS = ((128, 1), (512, 4), (2048, 16))
N_BUCKETS = 32
MAX_DISTANCE = 2048
N_EXPERTS = 16
N_GROUPS = 4
EXPERTS_PER_GROUP = N_EXPERTS // N_GROUPS
D_EXPERT = 512
EPS = 1e-5

LANES = 128
NEG = -1e30
INT_MIN = -(2 ** 31)
MASK_BUCKET = N_BUCKETS
VMEM_LIMIT = 56 * 1024 * 1024


def _cparams(*sem):
    return pltpu.CompilerParams(dimension_semantics=sem, vmem_limit_bytes=VMEM_LIMIT)


def _resident(shape, index_map):
    return pl.BlockSpec(shape, index_map, pipeline_mode=pl.Buffered(1))


def _t5_bucket(dist):
    n = jnp.maximum(dist, 0)
    max_exact = N_BUCKETS // 2
    nf = jnp.maximum(n, max_exact).astype(F32)
    large = max_exact + (jnp.log(nf / max_exact) / math.log(MAX_DISTANCE / max_exact)
                         * (N_BUCKETS - max_exact)).astype(I32)
    large = jnp.minimum(large, N_BUCKETS - 1)
    return jnp.where(n < max_exact, n, large)


def _layer_norm(v, g, b):
    mu = jnp.mean(v, axis=-1, keepdims=True)
    c = v - mu
    var = jnp.mean(c * c, axis=-1, keepdims=True)
    return c * lax.rsqrt(var + EPS) * g + b


def _rms_norm(v, g):
    return v * lax.rsqrt(jnp.mean(v * v, axis=-1, keepdims=True) + EPS) * g


def _bias_expand_body(bucket_ref, rb_ref, o_ref):
    bk = bucket_ref[0]
    for h in range(N_HEADS):
        acc = jnp.where(bk == MASK_BUCKET, NEG, 0.0).astype(F32)
        for b in range(N_BUCKETS):
            acc = jnp.where(bk == b, rb_ref[b, h], acc)
        o_ref[0, h] = acc


def bias_expand(bucket, rel_bias):
    n, r, c = bucket.shape
    return pl.pallas_call(
        _bias_expand_body,
        grid=(n,),
        in_specs=[pl.BlockSpec((1, r, c), lambda i: (i, 0, 0)),
                  pl.BlockSpec(memory_space=pltpu.SMEM)],
        out_specs=pl.BlockSpec((1, N_HEADS, r, c), lambda i: (i, 0, 0, 0)),
        out_shape=jax.ShapeDtypeStruct((n, N_HEADS, r, c), F32),
        compiler_params=_cparams("arbitrary"),
        name="bias_expand",
    )(bucket, rel_bias)


def _qkv_body(x_ref, w_ref, *refs, dils, scaled_blocks, scale):
    n_out = len(dils)
    out_refs, lvl_refs = refs[:n_out], refs[n_out:]
    tm = x_ref.shape[0]
    acc = jnp.dot(x_ref[...], w_ref[...], preferred_element_type=F32)
    acc = acc * jnp.where(pl.program_id(0) < scaled_blocks, scale, 1.0).astype(F32)
    n_slab = acc.shape[1] // LANES
    out_refs[0][0, 0] = acc.astype(out_refs[0].dtype)
    for s in range(n_slab):
        lvl_refs[0][s, 0] = acc[:, s * LANES:(s + 1) * LANES]
    for idx in range(1, n_out):
        dil, prev = dils[idx], dils[idx - 1]
        f = dil // prev
        for c in range(dil):
            cols = []
            for s in range(n_slab):
                blk = lvl_refs[idx - 1][s, c % prev, pl.ds(c // prev, tm // dil, stride=f), :]
                if idx + 1 < n_out:
                    lvl_refs[idx][s, c] = blk
                cols.append(blk)
            out_refs[idx][0, c] = jnp.concatenate(cols, axis=1).astype(out_refs[idx].dtype)


def qkv_projection(x, w, bsz, seq, dils, tm, tn, scaled_blocks, scale):
    m, k = x.shape
    n = w.shape[1]
    tpb = seq // tm
    assert dils[0] == 1 and all(b % a == 0 for a, b in zip(dils, dils[1:])), dils
    out_spec = lambda dil: pl.BlockSpec((1, dil, tm // dil, tn),
                                        lambda j, i: (i // tpb, 0, i % tpb, j))
    return pl.pallas_call(
        functools.partial(_qkv_body, dils=dils, scaled_blocks=scaled_blocks, scale=scale),
        grid=(n // tn, m // tm),
        in_specs=[pl.BlockSpec((tm, k), lambda j, i: (i, 0)),
                  pl.BlockSpec((k, tn), lambda j, i: (0, j))],
        out_specs=[out_spec(dil) for dil in dils],
        out_shape=[jax.ShapeDtypeStruct((bsz, dil, seq // dil, n), BF16) for dil in dils],
        scratch_shapes=[pltpu.VMEM((tn // LANES, dil, tm // dil, LANES), F32)
                        for dil in dils[:-1]],
        compiler_params=_cparams("arbitrary", "arbitrary"),
        name="qkv_projection",
    )(x, w)


def _dsa_pro_body(x_ref, win_ref, qn_ref, kvn_ref, wuq_ref, wuqi_ref, wuk_ref,
                  qlat_ref, qidx_ref, widx_ref, ckv_ref, kidx_ref):
    proj = jnp.dot(x_ref[...], win_ref[...], preferred_element_type=F32)
    c_q = _rms_norm(proj[:, :Q_LORA], qn_ref[...]).astype(BF16)
    c_kv = _rms_norm(proj[:, Q_LORA:Q_LORA + KV_LORA], kvn_ref[...])
    ckv_ref[...] = c_kv.astype(BF16)
    kidx_ref[...] = proj[:, 768:768 + IDX_DIM].astype(BF16)
    widx_ref[...] = (proj[:, 896:1024] * (IDX_HEADS ** -0.5)).T[:IDX_HEADS]
    q = jnp.dot(c_q, wuq_ref[...], preferred_element_type=F32)
    for h in range(N_HEADS):
        qh = q[:, h * HEAD_DIM:(h + 1) * HEAD_DIM].astype(BF16)
        ql = jnp.dot(qh, wuk_ref[h], preferred_element_type=F32) * (HEAD_DIM ** -0.5)
        qlat_ref[h] = ql.astype(BF16)
    qi = jnp.dot(c_q, wuqi_ref[...], preferred_element_type=F32) * (IDX_DIM ** -0.5)
    for h in range(IDX_HEADS):
        qidx_ref[h] = qi[:, h * IDX_DIM:(h + 1) * IDX_DIM].astype(BF16)


def dsa_prologue(xb, w_in_p, q_norm, kv_norm, w_uq, w_uq_idx, w_uk, tm):
    n, d = xb.shape
    hw = N_HEADS * HEAD_DIM
    return pl.pallas_call(
        _dsa_pro_body,
        grid=(n // tm,),
        in_specs=[pl.BlockSpec((tm, d), lambda i: (i, 0)),
                  _resident((d, 1024), lambda i: (0, 0)),
                  _resident((1, Q_LORA), lambda i: (0, 0)),
                  _resident((1, KV_LORA), lambda i: (0, 0)),
                  _resident((Q_LORA, hw), lambda i: (0, 0)),
                  _resident((Q_LORA, IDX_HEADS * IDX_DIM), lambda i: (0, 0)),
                  _resident((N_HEADS, HEAD_DIM, KV_LORA), lambda i: (0, 0, 0))],
        out_specs=[pl.BlockSpec((N_HEADS, tm, KV_LORA), lambda i: (0, i, 0)),
                   pl.BlockSpec((IDX_HEADS, tm, IDX_DIM), lambda i: (0, i, 0)),
                   pl.BlockSpec((IDX_HEADS, tm), lambda i: (0, i)),
                   pl.BlockSpec((tm, KV_LORA), lambda i: (i, 0)),
                   pl.BlockSpec((tm, IDX_DIM), lambda i: (i, 0))],
        out_shape=[jax.ShapeDtypeStruct((N_HEADS, n, KV_LORA), BF16),
                   jax.ShapeDtypeStruct((IDX_HEADS, n, IDX_DIM), BF16),
                   jax.ShapeDtypeStruct((IDX_HEADS, n), F32),
                   jax.ShapeDtypeStruct((n, KV_LORA), BF16),
                   jax.ShapeDtypeStruct((n, IDX_DIM), BF16)],
        compiler_params=_cparams("arbitrary"),
        name="dsa_prologue",
    )(xb, w_in_p, q_norm, kv_norm, w_uq, w_uq_idx, w_uk)


DSA_TQ = 128
DSA_TK = 256


def _dsa_attn_body(qlat_ref, qidx_ref, widxt_ref, ckv_ref, kidx_ref, tab_ref, wuv_ref, o_ref,
                   keys_ref, keyst_ref, acc_ref, m_ref, pstar_ref,
                   s0_ref, s1_ref, p0_ref, p1_ref, a0_ref, a1_ref,
                   *, seq, topk, n_tab):
    tq, tk = DSA_TQ, DSA_TK
    i = pl.program_id(1)
    q0 = i * tq
    nk = (q0 + tq + tk - 1) // tk
    qpos = q0 + lax.broadcasted_iota(I32, (tq, tk), 0)
    col = lax.broadcasted_iota(I32, (tq, tk), 1)
    qpos_t = q0 + lax.broadcasted_iota(I32, (tk, tq), 1)
    krow_t = lax.broadcasted_iota(I32, (tk, tq), 0)

    def dup(v):
        return jnp.concatenate([v] * (tk // LANES), axis=1)

    def score_tile(j, carry):
        k0 = pl.multiple_of(j * tk, tk)
        kt = kidx_ref[pl.ds(k0, tk), :]
        rel = lax.dot_general(kt, qidx_ref[...].reshape(IDX_HEADS * tq, IDX_DIM),
                              (((1,), (1,)), ((), ())), preferred_element_type=F32)
        s = jnp.zeros((tk, tq), F32)
        for h in range(IDX_HEADS):
            s = s + jnp.maximum(rel[:, h * tq:(h + 1) * tq], 0.0) * widxt_ref[h:h + 1, :]
        bits = pltpu.bitcast(s, I32)
        key = bits ^ ((bits >> 31) & 0x7FFFFFFF)
        key = jnp.where(k0 + krow_t <= qpos_t, key, INT_MIN)
        keyst_ref[j] = key
        keys_ref[j] = key.T
        return carry

    def score_pair(jj, carry):
        score_tile(2 * jj, carry)
        return score_tile(2 * jj + 1, carry)

    lax.fori_loop(0, (nk + 1) // 2, score_pair, 0)

    def count(pred):
        def body(jj, acc):
            for j in (2 * jj, 2 * jj + 1):
                ind = jnp.where(pred(keyst_ref[j], j), 1.0, 0.0)
                parts = [ind[r * 8:(r + 1) * 8, :] for r in range(tk // 8)]
                while len(parts) > 1:
                    parts = [a + b for a, b in zip(parts[::2], parts[1::2])]
                acc = acc + parts[0]
            return acc
        acc = lax.fori_loop(0, (nk + 1) // 2, body, jnp.zeros((8, tq), F32))
        return jnp.sum(acc, axis=0, keepdims=True)

    def bit_step(t, carry):
        v, cge = carry
        cand_u = v | jnp.left_shift(jnp.int32(1), 31 - t)
        cand = cand_u ^ INT_MIN
        cnt = count(lambda key, j: key >= cand)
        take = cnt >= topk
        return jnp.where(take, cand_u, v), jnp.where(take, cnt, cge)

    v, cge = lax.fori_loop(0, 32, bit_step,
                           (jnp.zeros((1, tq), I32), jnp.zeros((1, tq), F32)))
    thr_t = v ^ INT_MIN

    pstar_ref[...] = jnp.full((1, tq), seq, I32)
    surplus = jnp.where((v != 0) & (cge > topk), 1.0, 0.0)

    @pl.when(jnp.max(surplus) > 0.0)
    def _():
        cgt = count(lambda key, j: key > thr_t)
        need = topk - cgt

        def pos_step(t, p):
            cand = p | jnp.left_shift(jnp.int32(1), (seq.bit_length() - 1) - t)
            before = count(lambda key, j: (key == thr_t) & (j * tk + krow_t < cand))
            return jnp.where(before < need, cand, p)

        p = lax.fori_loop(0, seq.bit_length(), pos_step, jnp.zeros((1, tq), I32))
        pstar_ref[...] = jnp.where(surplus > 0.0, p, seq)

    def per_row(v_t):
        return dup(jnp.broadcast_to(v_t, (LANES, tq)).T)

    thr_w = per_row(thr_t)
    pstar_w = per_row(pstar_ref[...])

    acc_ref[...] = jnp.zeros_like(acc_ref)
    m_ref[...] = jnp.full_like(m_ref, NEG)
    dq = q0 // LANES

    n_tiles = seq // tk
    qlat2d = lambda: qlat_ref[...].reshape(N_HEADS * tq, KV_LORA)

    def kv_tile(j):
        return ckv_ref[pl.ds(pl.multiple_of(jnp.clip(j, 0, n_tiles - 1) * tk, tk), tk), :]

    def logits(j):
        return lax.dot_general(qlat2d(), kv_tile(j), (((1,), (1,)), ((), ())),
                               preferred_element_type=F32)

    s_bufs, p_bufs, alpha_bufs = (s0_ref, s1_ref), (p0_ref, p1_ref), (a0_ref, a1_ref)
    s0_ref[...] = logits(0)
    p1_ref[...] = jnp.zeros(p1_ref.shape, BF16)
    a1_ref[...] = jnp.ones(a1_ref.shape, F32)

    def attn_step(j, cur):
        prv = 1 - cur
        s_ref, p_ref, alpha_ref = s_bufs[cur], p_bufs[cur], alpha_bufs[cur]
        s_bufs[prv][...] = logits(j + 1)

        kpos = j * tk + col
        key = keys_ref[jnp.minimum(j, nk - 1)]
        sel = (kpos <= qpos) & ((key > thr_w) | ((key == thr_w) & (kpos <= pstar_w)))
        maskadd = jnp.where(sel, 0.0, NEG)
        dbase = dq - j * (tk // LANES) + 1
        didx = [jnp.clip(dbase - c, 0, n_tab - 1) for c in range(tk // LANES)]
        for h in range(N_HEADS):
            rows = slice(h * tq, (h + 1) * tq)
            bias = jnp.concatenate([tab_ref[d, h] for d in didx], axis=1)
            s = s_ref[rows, :] + bias + maskadd
            s_ref[rows, :] = s
            m_old = m_ref[rows, :]
            m_new = jnp.maximum(m_old, jnp.max(s, axis=1, keepdims=True))
            alpha_ref[rows, :] = jnp.exp(m_old - m_new)
            m_ref[rows, :] = m_new

        kv_ext = jnp.concatenate([kv_tile(j - 1), jnp.ones((tk, LANES), BF16)], axis=1)
        pv = jnp.dot(p_bufs[prv][...], kv_ext, preferred_element_type=F32)
        acc_ref[...] = alpha_bufs[prv][...] * acc_ref[...] + pv

        for h in range(N_HEADS):
            rows = slice(h * tq, (h + 1) * tq)
            p_ref[rows, :] = jnp.exp(s_ref[rows, :] - m_ref[rows, :]).astype(BF16)

    def attn_pair(jj, carry):
        attn_step(2 * jj, 0)
        attn_step(2 * jj + 1, 1)
        return carry

    n_pairs = (nk + 1) // 2
    lax.fori_loop(0, n_pairs, attn_pair, 0)
    kv_ext = jnp.concatenate([kv_tile(2 * n_pairs - 1), jnp.ones((tk, LANES), BF16)], axis=1)
    acc_ref[...] = a1_ref[...] * acc_ref[...] + jnp.dot(p1_ref[...], kv_ext,
                                                        preferred_element_type=F32)

    outs = []
    for h in range(N_HEADS):
        rows = slice(h * tq, (h + 1) * tq)
        inv_l = 1.0 / acc_ref[rows, KV_LORA:]
        o_lat = acc_ref[rows, :KV_LORA] * jnp.concatenate([inv_l] * (KV_LORA // LANES), axis=1)
        outs.append(jnp.dot(o_lat.astype(BF16), wuv_ref[h], preferred_element_type=F32))
    o_ref[...] = jnp.concatenate(outs, axis=1).astype(o_ref.dtype)


def dsa_attention(qlat, qidx, widx, ckv, kidx, tab, w_uv, bsz, seq):
    tq, tk = DSA_TQ, DSA_TK
    nq = seq // tq
    n = bsz * seq
    n_tab = tab.shape[0]
    topk = min(IDX_TOPK_MAX, seq // 4)
    body = functools.partial(_dsa_attn_body, seq=seq, topk=topk, n_tab=n_tab)
    return pl.pallas_call(
        body,
        grid=(bsz, nq),
        in_specs=[pl.BlockSpec((N_HEADS, tq, KV_LORA), lambda b, i: (0, b * nq + i, 0)),
                  pl.BlockSpec((IDX_HEADS, tq, IDX_DIM), lambda b, i: (0, b * nq + i, 0)),
                  pl.BlockSpec((IDX_HEADS, tq), lambda b, i: (0, b * nq + i)),
                  pl.BlockSpec((seq, KV_LORA), lambda b, i: (b, 0)),
                  pl.BlockSpec((seq, IDX_DIM), lambda b, i: (b, 0)),
                  _resident(tab.shape, lambda b, i: (0, 0, 0, 0)),
                  _resident((N_HEADS, KV_LORA, HEAD_DIM), lambda b, i: (0, 0, 0))],
        out_specs=pl.BlockSpec((tq, N_HEADS * HEAD_DIM), lambda b, i: (b * nq + i, 0)),
        out_shape=jax.ShapeDtypeStruct((n, N_HEADS * HEAD_DIM), BF16),
        scratch_shapes=[pltpu.VMEM((seq // tk, tq, tk), I32),
                        pltpu.VMEM((seq // tk, tk, tq), I32),
                        pltpu.VMEM((N_HEADS * tq, KV_LORA + LANES), F32),
                        pltpu.VMEM((N_HEADS * tq, 1), F32),
                        pltpu.VMEM((1, tq), I32),
                        pltpu.VMEM((N_HEADS * tq, tk), F32),
                        pltpu.VMEM((N_HEADS * tq, tk), F32),
                        pltpu.VMEM((N_HEADS * tq, tk), BF16),
                        pltpu.VMEM((N_HEADS * tq, tk), BF16),
                        pltpu.VMEM((N_HEADS * tq, 1), F32),
                        pltpu.VMEM((N_HEADS * tq, 1), F32)],
        compiler_params=_cparams("arbitrary", "arbitrary"),
        name="dsa_attention",
    )(qlat, qidx, widx, ckv, kidx, tab, w_uv)


def dsa_bias_buckets(seq):
    far = np.maximum(np.arange(seq), N_BUCKETS // 2).astype(np.float64)
    far_bucket = np.floor(np.log(far / (N_BUCKETS // 2)) / math.log(MAX_DISTANCE / (N_BUCKETS // 2))
                          * (N_BUCKETS - N_BUCKETS // 2))
    unsat = np.nonzero(far_bucket < N_BUCKETS - 1 - N_BUCKETS // 2)[0]
    n_sat = int(unsat[-1]) + 1 if unsat.size else 0
    n_tab = min(seq // LANES + 1, -(-(n_sat + LANES - 1) // LANES) + 2)
    t = np.arange(n_tab)[:, None, None]
    qi = np.arange(LANES)[None, :, None]
    ki = np.arange(LANES)[None, None, :]
    dist = np.maximum(LANES * (t - 1) + qi - ki, 0)
    return _t5_bucket(jnp.asarray(dist, I32))


def _dil_body(q_ref, kp_ref, km_ref, vp_ref, vm_ref, bias_ref, o_ref, lse_ref, *, look):
    nblk = pl.program_id(2)
    tq = q_ref.shape[2]
    ncol = look + tq
    colv = lax.broadcasted_iota(I32, (1, ncol), 1)
    first = jnp.where((nblk == 0) & (colv < look), NEG, 0.0)
    lse_ref[0, 0] = jnp.zeros(lse_ref.shape[2:], F32)
    ones = jnp.ones((ncol, HEAD_DIM), BF16)
    outs = []
    for h in range(N_HEADS):
        sl = slice(h * HEAD_DIM, (h + 1) * HEAD_DIM)
        q = q_ref[0, 0, :, sl]
        k = jnp.concatenate([kp_ref[0, 0, :, sl], km_ref[0, 0, :, sl]], axis=0)
        v = jnp.concatenate([vp_ref[0, 0, :, sl], vm_ref[0, 0, :, sl]], axis=0)
        s = lax.dot_general(q, k, (((1,), (1,)), ((), ())), preferred_element_type=F32)
        s = s + bias_ref[0, h] + first
        m = jnp.max(s, axis=1, keepdims=True)
        p = jnp.exp(s - m)
        ov = jnp.dot(p.astype(BF16), jnp.concatenate([v, ones], axis=1),
                     preferred_element_type=F32)
        l = ov[:, HEAD_DIM:]
        outs.append(ov[:, :HEAD_DIM] * (1.0 / l))
        lse_ref[0, 0, :, h:h + 1] = m + jnp.log(l[:, :1])
    o_ref[0, 0] = jnp.concatenate(outs, axis=1).astype(o_ref.dtype)


def dilated_branch(qkv, bias, look):
    bsz, dil, ls, _ = qkv.shape
    hw = N_HEADS * HEAD_DIM
    tq = bias.shape[2]
    nb = ls // tq
    r = tq // look
    prev = lambda part: (lambda b, c, n: (b, c, jnp.maximum(n * r - 1, 0), part))
    main = lambda part: (lambda b, c, n: (b, c, n, part))
    return pl.pallas_call(
        functools.partial(_dil_body, look=look),
        grid=(bsz, dil, nb),
        in_specs=[pl.BlockSpec((1, 1, tq, hw), main(0)),
                  pl.BlockSpec((1, 1, look, hw), prev(1)),
                  pl.BlockSpec((1, 1, tq, hw), main(1)),
                  pl.BlockSpec((1, 1, look, hw), prev(2)),
                  pl.BlockSpec((1, 1, tq, hw), main(2)),
                  _resident(bias.shape, lambda b, c, n: (0, 0, 0, 0))],
        out_specs=[pl.BlockSpec((1, 1, tq, hw), lambda b, c, n: (b, c, n, 0)),
                   pl.BlockSpec((1, 1, tq, LANES), lambda b, c, n: (b, c, n, 0))],
        out_shape=[jax.ShapeDtypeStruct((bsz, dil, ls, hw), BF16),
                   jax.ShapeDtypeStruct((bsz, dil, ls, LANES), F32)],
        compiler_params=_cparams("arbitrary", "arbitrary", "arbitrary"),
        name=f"dilated_d{dil}",
    )(qkv, qkv, qkv, qkv, qkv, bias)


def dilated_bias_buckets(tq, look, dil):
    rel = np.arange(tq)[:, None] + look - np.arange(look + tq)[None, :]
    band = (rel >= 0) & (rel <= look)
    bucket = _t5_bucket(jnp.asarray(dil * np.maximum(rel, 0), I32))
    return jnp.where(jnp.asarray(band), bucket, MASK_BUCKET)[None]


def _post_attn_body(*refs, dils, alpha):
    n_parts = len(dils)
    merge = n_parts > 1
    o_refs = refs[:n_parts]
    lse_refs = refs[n_parts:2 * n_parts] if merge else ()
    base = n_parts + len(lse_refs)
    wo_ref, x_ref, g_ref, b_ref, xo_ref, xb_ref = refs[base:base + 6]
    scratch = list(refs[base + 6:])
    tm = x_ref.shape[0]
    head = lambda h: slice(h * HEAD_DIM, (h + 1) * HEAD_DIM)
    if not merge:
        o = o_refs[0][0, 0]
    else:
        lses, o_heads = [], []
        for part, dil in enumerate(dils):
            if dil == 1:
                lses.append(lse_refs[part][0, 0])
                o_heads.append(lambda h, r=o_refs[part]: r[0, 0, :, head(h)].astype(F32))
                continue
            o_scr, l_scr = scratch.pop(0), scratch.pop(0)
            rows = tm // dil
            for c in range(dil):
                l_scr[pl.ds(c, rows, stride=dil), :] = lse_refs[part][0, c]
                for h in range(N_HEADS):
                    o_scr[h, pl.ds(c, rows, stride=dil), :] = (
                        o_refs[part][0, c, :, head(h)].astype(F32))
            lses.append(l_scr[...])
            o_heads.append(lambda h, r=o_scr: r[h])
        mx = functools.reduce(jnp.maximum, lses)
        ws = [jnp.exp(v - mx) for v in lses]
        inv = 1.0 / functools.reduce(lambda a, b: a + b, ws)
        ws = [w * inv for w in ws]
        cols = []
        for h in range(N_HEADS):
            acc = None
            for part in range(n_parts):
                term = ws[part][:, h:h + 1] * o_heads[part](h)
                acc = term if acc is None else acc + term
            cols.append(acc)
        o = jnp.concatenate(cols, axis=1).astype(BF16)
    hproj = jnp.dot(o, wo_ref[...], preferred_element_type=F32)
    y = _layer_norm(alpha * x_ref[...] + hproj, g_ref[...], b_ref[...])
    xo_ref[...] = y
    xb_ref[...] = y.astype(BF16)


def post_attention(o_parts, lse_parts, w_o, x, g, b, alpha, tm):
    n, d = x.shape
    hw = N_HEADS * HEAD_DIM
    dils = tuple(o.shape[1] for o in o_parts)
    seq = o_parts[0].shape[1] * o_parts[0].shape[2]
    tpb = seq // tm
    row = lambda width: pl.BlockSpec((tm, width), lambda i: (i, 0))
    by_class = lambda dil, width: pl.BlockSpec((1, dil, tm // dil, width),
                                               lambda i: (i // tpb, 0, i % tpb, 0))
    in_specs = ([by_class(dil, hw) for dil in dils]
                + [by_class(dil, LANES) for dil in dils[:len(lse_parts)]]
                + [_resident((hw, d), lambda i: (0, 0)), row(d),
                   _resident((1, d), lambda i: (0, 0)), _resident((1, d), lambda i: (0, 0))])
    scratch = []
    for dil in dils:
        if dil > 1:
            scratch += [pltpu.VMEM((N_HEADS, tm, HEAD_DIM), F32), pltpu.VMEM((tm, LANES), F32)]
    return pl.pallas_call(
        functools.partial(_post_attn_body, dils=dils, alpha=alpha),
        grid=(n // tm,),
        in_specs=in_specs,
        out_specs=[row(d), row(d)],
        out_shape=[jax.ShapeDtypeStruct((n, d), F32), jax.ShapeDtypeStruct((n, d), BF16)],
        scratch_shapes=scratch,
        compiler_params=_cparams("arbitrary"),
        name=f"post_attention_{len(dils)}",
    )(*o_parts, *lse_parts, w_o, x, g, b)


def _router_body(x_ref, rwt_ref, rb_ref, route_ref, gates_ref, counts_ref, carry_ref):
    tm = x_ref.shape[0]

    @pl.when(pl.program_id(0) == 0)
    def _():
        carry_ref[...] = jnp.zeros_like(carry_ref)

    logits = lax.dot_general(rwt_ref[...], x_ref[...], (((1,), (1,)), ((), ())),
                             preferred_element_type=F32,
                             precision=lax.Precision.HIGHEST)
    rows = [logits[e:e + 1, :] for e in range(N_EXPERTS)]
    mx = functools.reduce(jnp.maximum, rows)
    ex = [jnp.exp(r - mx) for r in rows]
    inv = 1.0 / functools.reduce(lambda a, b: a + b, ex)
    probs = [e * inv for e in ex]
    sel = [probs[e] + rb_ref[e] for e in range(N_EXPERTS)]

    zero_i = jnp.zeros((1, tm), I32)
    best_t = None
    for g in range(N_GROUPS):
        vals = sel[g * EXPERTS_PER_GROUP:(g + 1) * EXPERTS_PER_GROUP]
        prb = probs[g * EXPERTS_PER_GROUP:(g + 1) * EXPERTS_PER_GROUP]
        top, top_i, top_p = vals[0], zero_i, prb[0]
        for j in range(1, EXPERTS_PER_GROUP):
            c = vals[j] > top
            top = jnp.where(c, vals[j], top)
            top_i = jnp.where(c, j, top_i)
            top_p = jnp.where(c, prb[j], top_p)
        sec = jnp.full((1, tm), -jnp.inf, F32)
        sec_i, sec_p = zero_i, jnp.zeros((1, tm), F32)
        for j in range(EXPERTS_PER_GROUP):
            c = (vals[j] > sec) & (top_i != j)
            sec = jnp.where(c, vals[j], sec)
            sec_i = jnp.where(c, j, sec_i)
            sec_p = jnp.where(c, prb[j], sec_p)
        tot = top + sec
        if best_t is None:
            best_t = tot
            e0, e1 = top_i + g * EXPERTS_PER_GROUP, sec_i + g * EXPERTS_PER_GROUP
            p0, p1 = top_p, sec_p
        else:
            c = tot > best_t
            best_t = jnp.where(c, tot, best_t)
            e0 = jnp.where(c, top_i + g * EXPERTS_PER_GROUP, e0)
            e1 = jnp.where(c, sec_i + g * EXPERTS_PER_GROUP, e1)
            p0 = jnp.where(c, top_p, p0)
            p1 = jnp.where(c, sec_p, p1)
    ginv = 1.0 / (p0 + p1)
    g0, g1 = p0 * ginv, p1 * ginv
    gates_ref[...] = jnp.concatenate([g0, g1, jnp.zeros((LANES - 2, tm), F32)], axis=0).T

    assign = jnp.concatenate([jnp.where((e0 == e) | (e1 == e), 1.0, 0.0)
                              for e in range(N_EXPERTS)], axis=0).astype(BF16)
    upper = jnp.where(lax.broadcasted_iota(I32, (tm, tm), 0)
                      <= lax.broadcasted_iota(I32, (tm, tm), 1), 1.0, 0.0).astype(BF16)
    prefix = jnp.dot(assign, upper, preferred_element_type=F32) + carry_ref[...]
    rank0 = jnp.zeros((1, tm), F32)
    rank1 = jnp.zeros((1, tm), F32)
    for e in range(N_EXPERTS):
        rank0 = jnp.where(e0 == e, prefix[e:e + 1, :], rank0)
        rank1 = jnp.where(e1 == e, prefix[e:e + 1, :], rank1)
    total = prefix[:, tm - 1:tm]
    carry_ref[...] = total
    counts_ref[...] = jnp.broadcast_to(total, counts_ref.shape).astype(I32)
    route_ref[...] = jnp.concatenate(
        [e0, e1, (rank0 - 1.0).astype(I32), (rank1 - 1.0).astype(I32), jnp.zeros((4, tm), I32)],
        axis=0)


def router(x, router_wt, router_b, tm):
    n, d = x.shape
    return pl.pallas_call(
        _router_body,
        grid=(n // tm,),
        in_specs=[pl.BlockSpec((tm, d), lambda i: (i, 0)),
                  _resident((N_EXPERTS, d), lambda i: (0, 0)),
                  pl.BlockSpec(memory_space=pltpu.SMEM)],
        out_specs=[pl.BlockSpec((8, tm), lambda i: (0, i)),
                   pl.BlockSpec((tm, LANES), lambda i: (i, 0)),
                   pl.BlockSpec((N_EXPERTS, LANES), lambda i: (0, 0))],
        out_shape=[jax.ShapeDtypeStruct((8, n), I32),
                   jax.ShapeDtypeStruct((n, LANES), F32),
                   jax.ShapeDtypeStruct((N_EXPERTS, LANES), I32)],
        scratch_shapes=[pltpu.VMEM((N_EXPERTS, 1), F32)],
        compiler_params=_cparams("arbitrary"),
        name="router",
    )(x, router_wt, router_b)


EXPERT_TILE = 512
COMBINE_TILE = 256


def _row_copy(src_hbm, row, dst, dst_row, sem):
    return pltpu.make_async_copy(src_hbm.at[pl.ds(row, 1), :], dst.at[pl.ds(dst_row, 1), :], sem)


def _experts_body(te_ref, nu_ref, src_next_ref, src_first_ref, x_hbm, wg_ref, wu_ref, wd_ref,
                  ys_ref, xbuf, sem, wgb, wub, wdb):
    i = pl.program_id(0)
    n_used = nu_ref[0]
    slot = i % 2
    tmx = ys_ref.shape[0]

    def issue(src_ref, dst_slot):
        for r in range(tmx):
            _row_copy(x_hbm, src_ref[0, 0, r], xbuf.at[dst_slot], r, sem.at[dst_slot]).start()

    @pl.when(i == 0)
    def _():
        issue(src_first_ref, 0)

    @pl.when(i + 1 < n_used)
    def _():
        issue(src_next_ref, 1 - slot)

    @pl.when(i < n_used)
    def _():
        def wait_row(r, c):
            _row_copy(x_hbm, 0, xbuf.at[slot], r, sem.at[slot]).wait()
            return c
        lax.fori_loop(0, tmx, wait_row, 0, unroll=8)

        @pl.when((i == 0) | (te_ref[i] != te_ref[jnp.maximum(i - 1, 0)]))
        def _():
            wgb[...] = wg_ref[0, 0].astype(BF16)
            wub[...] = wu_ref[0, 0].astype(BF16)
            wdb[...] = wd_ref[0, 0].astype(BF16)

        xb = xbuf[slot].astype(BF16)
        gate = jnp.dot(xb, wgb[...], preferred_element_type=F32)
        up = jnp.dot(xb, wub[...], preferred_element_type=F32)
        hidden = (gate * jax.nn.sigmoid(gate)) * up
        ys_ref[...] = jnp.dot(hidden.astype(BF16), wdb[...], preferred_element_type=F32)

    @pl.when(i >= n_used)
    def _():
        ys_ref[...] = jnp.zeros_like(ys_ref)


def moe_experts(x, src_tiles, tile_expert, n_used, w_gate, w_up, w_down, layer):
    n, d = x.shape
    n_tiles = src_tiles.shape[0]
    tmx = EXPERT_TILE
    smem_tile = lambda index_map: pl.BlockSpec((1, 1, tmx), index_map, memory_space=pltpu.SMEM)
    by_expert = lambda shape: pl.BlockSpec((1,) + shape, lambda i, te, nu: (layer, te[i], 0, 0))
    grid_spec = pltpu.PrefetchScalarGridSpec(
        num_scalar_prefetch=2,
        grid=(n_tiles,),
        in_specs=[smem_tile(lambda i, te, nu: (jnp.minimum(i + 1, n_tiles - 1), 0, 0)),
                  smem_tile(lambda i, te, nu: (0, 0, 0)),
                  pl.BlockSpec(memory_space=pl.ANY),
                  by_expert((1, d, D_EXPERT)), by_expert((1, d, D_EXPERT)),
                  by_expert((1, D_EXPERT, d))],
        out_specs=pl.BlockSpec((tmx, d), lambda i, te, nu: (i, 0)),
        scratch_shapes=[pltpu.VMEM((2, tmx, d), F32),
                        pltpu.SemaphoreType.DMA((2,)),
                        pltpu.VMEM((d, D_EXPERT), BF16),
                        pltpu.VMEM((d, D_EXPERT), BF16),
                        pltpu.VMEM((D_EXPERT, d), BF16)])
    return pl.pallas_call(
        _experts_body,
        grid_spec=grid_spec,
        out_shape=jax.ShapeDtypeStruct((n_tiles * tmx, d), F32),
        compiler_params=_cparams("arbitrary"),
        name="moe_experts",
    )(tile_expert, n_used, src_tiles, src_tiles, x, w_gate, w_up, w_down)


def _combine_body(pos_next_ref, pos_first_ref, ys_hbm, gates_ref, x_ref, g_ref, b_ref,
                  xo_ref, xbo_ref, ybuf, sem, *, alpha):
    i = pl.program_id(0)
    n_steps = pl.num_programs(0)
    slot = i % 2
    tm = x_ref.shape[0]

    def issue(pos_ref, dst_slot):
        for k in range(2):
            for r in range(tm):
                _row_copy(ys_hbm, pos_ref[0, k, r], ybuf.at[dst_slot, k], r,
                          sem.at[dst_slot]).start()

    @pl.when(i == 0)
    def _():
        issue(pos_first_ref, 0)

    @pl.when(i + 1 < n_steps)
    def _():
        issue(pos_next_ref, 1 - slot)

    def wait_row(r, c):
        for k in range(2):
            _row_copy(ys_hbm, 0, ybuf.at[slot, k], r, sem.at[slot]).wait()
        return c
    lax.fori_loop(0, tm, wait_row, 0, unroll=8)

    gates = gates_ref[...]
    y = gates[:, 0:1] * ybuf[slot, 0] + gates[:, 1:2] * ybuf[slot, 1]
    out = _layer_norm(alpha * x_ref[...] + y, g_ref[...], b_ref[...])
    xo_ref[...] = out
    xbo_ref[...] = out.astype(BF16)


def moe_combine(ys, pos_tiles, gates, x, g, b, alpha):
    n, d = x.shape
    tm = COMBINE_TILE
    n_steps = n // tm
    smem_tile = lambda index_map: pl.BlockSpec((1, 2, tm), index_map, memory_space=pltpu.SMEM)
    row = lambda width: pl.BlockSpec((tm, width), lambda i: (i, 0))
    return pl.pallas_call(
        functools.partial(_combine_body, alpha=alpha),
        grid=(n_steps,),
        in_specs=[smem_tile(lambda i: (jnp.minimum(i + 1, n_steps - 1), 0, 0)),
                  smem_tile(lambda i: (0, 0, 0)),
                  pl.BlockSpec(memory_space=pl.ANY),
                  row(LANES), row(d),
                  _resident((1, d), lambda i: (0, 0)), _resident((1, d), lambda i: (0, 0))],
        out_specs=[row(d), row(d)],
        out_shape=[jax.ShapeDtypeStruct((n, d), F32), jax.ShapeDtypeStruct((n, d), BF16)],
        scratch_shapes=[pltpu.VMEM((2, 2, tm, d), F32), pltpu.SemaphoreType.DMA((2,))],
        compiler_params=_cparams("arbitrary"),
        name="moe_combine",
    )(pos_tiles, pos_tiles, ys, gates, x, g, b)


def moe_layer(x, router_wt, router_b, w_gate, w_up, w_down, layer, g, b, alpha):
    n, d = x.shape
    tmx = EXPERT_TILE
    n_tiles = (2 * n) // tmx + N_EXPERTS
    route, gates, counts = router(x, router_wt, router_b, tm=512)
    counts = counts[:, 0]
    tiles_e = (counts + tmx - 1) // tmx
    tile_end = jnp.cumsum(tiles_e)
    row_off = (tile_end - tiles_e) * tmx
    pos0 = row_off[route[0]] + route[2]
    pos1 = row_off[route[1]] + route[3]
    tok = jnp.arange(n, dtype=I32)
    src = jnp.zeros((n_tiles * tmx,), I32).at[jnp.concatenate([pos0, pos1])].set(
        jnp.concatenate([tok, tok]), unique_indices=True)
    tile_expert = jnp.minimum(
        jnp.sum(jnp.arange(n_tiles, dtype=I32)[:, None] >= tile_end[None, :], axis=1),
        N_EXPERTS - 1).astype(I32)
    n_used = tile_end[-1:].astype(I32)
    ys = moe_experts(x, src.reshape(n_tiles, 1, tmx), tile_expert, n_used, w_gate, w_up, w_down,
                     layer)
    tm = COMBINE_TILE
    pos_tiles = jnp.stack([pos0, pos1]).reshape(2, n // tm, tm).transpose(1, 0, 2)
    return moe_combine(ys, pos_tiles, gates, x, g, b, alpha)


def kernel(x, rel_bias, router_w, router_b, a_w_in, a_q_norm, a_kv_norm, a_w_uq, a_w_uq_idx,
           a_w_uk, a_w_uv, a_w_o, b_w_in, b_w_o, ln1_g, ln1_b, ln2_g, ln2_b,
           moe_w_gate, moe_w_up, moe_w_down):
    bsz, seq, d = x.shape
    depth = ln1_g.shape[0]
    n = bsz * seq
    hw = N_HEADS * HEAD_DIM
    alpha = (2 * depth) ** 0.25

    x = x.reshape(n, d)
    xb = x.astype(BF16)
    router_wt = router_w.T

    dsa_tab = bias_expand(dsa_bias_buckets(seq), rel_bias)
    dil_bias = []
    for window, dil in DILATED_PATTERNS:
        look = window // dil
        tq = min(2 * look, seq // dil)
        dil_bias.append(bias_expand(dilated_bias_buckets(tq, look, dil), rel_bias))

    for i in range(depth):
        j = i // 2
        if i % 2 == 0:
            w_in = a_w_in[j]
            kv_end = Q_LORA + KV_LORA
            w_in_p = jnp.concatenate(
                [w_in[:, :kv_end + IDX_DIM], jnp.zeros((d, LANES - IDX_DIM), F32),
                 w_in[:, kv_end + IDX_DIM:], jnp.zeros((d, LANES - IDX_HEADS), F32)],
                axis=1).astype(BF16)
            qlat, qidx, widx, ckv, kidx = dsa_prologue(
                xb, w_in_p, a_q_norm[j][None], a_kv_norm[j][None], a_w_uq[j].astype(BF16),
                a_w_uq_idx[j].astype(BF16), a_w_uk[j].astype(BF16), tm=256)
            o = dsa_attention(qlat, qidx, widx, ckv, kidx, dsa_tab, a_w_uv[j].astype(BF16),
                              bsz, seq)
            o_parts, lse_parts, w_o = [o.reshape(bsz, 1, seq, hw)], [], a_w_o[j]
        else:
            dils = tuple(dil for _, dil in DILATED_PATTERNS)
            qkvs = qkv_projection(xb, b_w_in[j].astype(BF16), bsz, seq, dils, tm=512, tn=1024,
                                  scaled_blocks=hw // 1024, scale=HEAD_DIM ** -0.5)
            o_parts, lse_parts = [], []
            for (window, dil), qkv, bias in zip(DILATED_PATTERNS, qkvs, dil_bias):
                o_i, lse_i = dilated_branch(qkv, bias, window // dil)
                o_parts.append(o_i)
                lse_parts.append(lse_i)
            w_o = b_w_o[j]
        x, xb = post_attention(o_parts, lse_parts, w_o.astype(BF16), x, ln1_g[i][None],
                               ln1_b[i][None], alpha, tm=256)
        x, xb = moe_layer(x, router_wt, router_b, moe_w_gate, moe_w_up, moe_w_down, i,
                          ln2_g[i][None], ln2_b[i][None], alpha)
    return x.reshape(bsz, seq, d)
```

```python
import functools
import math

import numpy as np
import jax
import jax.numpy as jnp
from jax import lax
from jax.experimental import pallas as pl
from jax.experimental.pallas import tpu as pltpu

F32 = jnp.float32
BF16 = jnp.bfloat16
I32 = jnp.int32

N_HEADS = 16
HEAD_DIM = 128
Q_LORA = 512
KV_LORA = 256
IDX_HEADS = 16
IDX_DIM = 64
IDX_TOPK_MAX = 256
DILATED_PATTERNS = ((128, 1), (512, 4), (2048, 16))
N_BUCKETS = 32
MAX_DISTANCE = 2048
N_EXPERTS = 16
N_GROUPS = 4
EXPERTS_PER_GROUP = N_EXPERTS // N_GROUPS
D_EXPERT = 512
EPS = 1e-5

LANES = 128
NEG = -1e30
INT_MIN = -(2 ** 31)
MASK_BUCKET = N_BUCKETS
VMEM_LIMIT = 56 * 1024 * 1024


def _cparams(*sem):
    return pltpu.CompilerParams(dimension_semantics=sem, vmem_limit_bytes=VMEM_LIMIT)


def _resident(shape, index_map):
    return pl.BlockSpec(shape, index_map, pipeline_mode=pl.Buffered(1))


def _t5_bucket(dist):
    n = jnp.maximum(dist, 0)
    max_exact = N_BUCKETS // 2
    nf = jnp.maximum(n, max_exact).astype(F32)
    large = max_exact + (jnp.log(nf / max_exact) / math.log(MAX_DISTANCE / max_exact)
                         * (N_BUCKETS - max_exact)).astype(I32)
    large = jnp.minimum(large, N_BUCKETS - 1)
    return jnp.where(n < max_exact, n, large)


def _layer_norm(v, g, b):
    mu = jnp.mean(v, axis=-1, keepdims=True)
    c = v - mu
    var = jnp.mean(c * c, axis=-1, keepdims=True)
    return c * lax.rsqrt(var + EPS) * g + b


def _rms_norm(v, g):
    return v * lax.rsqrt(jnp.mean(v * v, axis=-1, keepdims=True) + EPS) * g


def _bias_expand_body(bucket_ref, rb_ref, o_ref):
    bk = bucket_ref[0]
    for h in range(N_HEADS):
        acc = jnp.where(bk == MASK_BUCKET, NEG, 0.0).astype(F32)
        for b in range(N_BUCKETS):
            acc = jnp.where(bk == b, rb_ref[b, h], acc)
        o_ref[0, h] = acc


def bias_expand(bucket, rel_bias):
    n, r, c = bucket.shape
    return pl.pallas_call(
        _bias_expand_body,
        grid=(n,),
        in_specs=[pl.BlockSpec((1, r, c), lambda i: (i, 0, 0)),
                  pl.BlockSpec(memory_space=pltpu.SMEM)],
        out_specs=pl.BlockSpec((1, N_HEADS, r, c), lambda i: (i, 0, 0, 0)),
        out_shape=jax.ShapeDtypeStruct((n, N_HEADS, r, c), F32),
        compiler_params=_cparams("arbitrary"),
        name="bias_expand",
    )(bucket, rel_bias)


def _qkv_body(x_ref, w_ref, *refs, dils, scaled_blocks, scale):
    n_out = len(dils)
    out_refs, lvl_refs = refs[:n_out], refs[n_out:]
    tm = x_ref.shape[0]
    acc = jnp.dot(x_ref[...], w_ref[...], preferred_element_type=F32)
    acc = acc * jnp.where(pl.program_id(0) < scaled_blocks, scale, 1.0).astype(F32)
    n_slab = acc.shape[1] // LANES
    out_refs[0][0, 0] = acc.astype(out_refs[0].dtype)
    for s in range(n_slab):
        lvl_refs[0][s, 0] = acc[:, s * LANES:(s + 1) * LANES]
    for idx in range(1, n_out):
        dil, prev = dils[idx], dils[idx - 1]
        f = dil // prev
        for c in range(dil):
            cols = []
            for s in range(n_slab):
                blk = lvl_refs[idx - 1][s, c % prev, pl.ds(c // prev, tm // dil, stride=f), :]
                if idx + 1 < n_out:
                    lvl_refs[idx][s, c] = blk
                cols.append(blk)
            out_refs[idx][0, c] = jnp.concatenate(cols, axis=1).astype(out_refs[idx].dtype)


def qkv_projection(x, w, bsz, seq, dils, tm, tn, scaled_blocks, scale):
    m, k = x.shape
    n = w.shape[1]
    tpb = seq // tm
    assert dils[0] == 1 and all(b % a == 0 for a, b in zip(dils, dils[1:])), dils
    out_spec = lambda dil: pl.BlockSpec((1, dil, tm // dil, tn),
                                        lambda j, i: (i // tpb, 0, i % tpb, j))
    return pl.pallas_call(
        functools.partial(_qkv_body, dils=dils, scaled_blocks=scaled_blocks, scale=scale),
        grid=(n // tn, m // tm),
        in_specs=[pl.BlockSpec((tm, k), lambda j, i: (i, 0)),
                  pl.BlockSpec((k, tn), lambda j, i: (0, j))],
        out_specs=[out_spec(dil) for dil in dils],
        out_shape=[jax.ShapeDtypeStruct((bsz, dil, seq // dil, n), BF16) for dil in dils],
        scratch_shapes=[pltpu.VMEM((tn // LANES, dil, tm // dil, LANES), F32)
                        for dil in dils[:-1]],
        compiler_params=_cparams("arbitrary", "arbitrary"),
        name="qkv_projection",
    )(x, w)


def _dsa_pro_body(x_ref, win_ref, qn_ref, kvn_ref, wuq_ref, wuqi_ref, wuk_ref,
                  qlat_ref, qidx_ref, widx_ref, ckv_ref, kidx_ref):
    proj = jnp.dot(x_ref[...], win_ref[...], preferred_element_type=F32)
    c_q = _rms_norm(proj[:, :Q_LORA], qn_ref[...]).astype(BF16)
    c_kv = _rms_norm(proj[:, Q_LORA:Q_LORA + KV_LORA], kvn_ref[...])
    ckv_ref[...] = c_kv.astype(BF16)
    kidx_ref[...] = proj[:, 768:768 + IDX_DIM].astype(BF16)
    widx_ref[...] = (proj[:, 896:1024] * (IDX_HEADS ** -0.5)).T[:IDX_HEADS]
    q = jnp.dot(c_q, wuq_ref[...], preferred_element_type=F32)
    for h in range(N_HEADS):
        qh = q[:, h * HEAD_DIM:(h + 1) * HEAD_DIM].astype(BF16)
        ql = jnp.dot(qh, wuk_ref[h], preferred_element_type=F32) * (HEAD_DIM ** -0.5)
        qlat_ref[h] = ql.astype(BF16)
    qi = jnp.dot(c_q, wuqi_ref[...], preferred_element_type=F32) * (IDX_DIM ** -0.5)
    for h in range(IDX_HEADS):
        qidx_ref[h] = qi[:, h * IDX_DIM:(h + 1) * IDX_DIM].astype(BF16)


def dsa_prologue(xb, w_in_p, q_norm, kv_norm, w_uq, w_uq_idx, w_uk, tm):
    n, d = xb.shape
    hw = N_HEADS * HEAD_DIM
    return pl.pallas_call(
        _dsa_pro_body,
        grid=(n // tm,),
        in_specs=[pl.BlockSpec((tm, d), lambda i: (i, 0)),
                  _resident((d, 1024), lambda i: (0, 0)),
                  _resident((1, Q_LORA), lambda i: (0, 0)),
                  _resident((1, KV_LORA), lambda i: (0, 0)),
                  _resident((Q_LORA, hw), lambda i: (0, 0)),
                  _resident((Q_LORA, IDX_HEADS * IDX_DIM), lambda i: (0, 0)),
                  _resident((N_HEADS, HEAD_DIM, KV_LORA), lambda i: (0, 0, 0))],
        out_specs=[pl.BlockSpec((N_HEADS, tm, KV_LORA), lambda i: (0, i, 0)),
                   pl.BlockSpec((IDX_HEADS, tm, IDX_DIM), lambda i: (0, i, 0)),
                   pl.BlockSpec((IDX_HEADS, tm), lambda i: (0, i)),
                   pl.BlockSpec((tm, KV_LORA), lambda i: (i, 0)),
                   pl.BlockSpec((tm, IDX_DIM), lambda i: (i, 0))],
        out_shape=[jax.ShapeDtypeStruct((N_HEADS, n, KV_LORA), BF16),
                   jax.ShapeDtypeStruct((IDX_HEADS, n, IDX_DIM), BF16),
                   jax.ShapeDtypeStruct((IDX_HEADS, n), F32),
                   jax.ShapeDtypeStruct((n, KV_LORA), BF16),
                   jax.ShapeDtypeStruct((n, IDX_DIM), BF16)],
        compiler_params=_cparams("arbitrary"),
        name="dsa_prologue",
    )(xb, w_in_p, q_norm, kv_norm, w_uq, w_uq_idx, w_uk)


DSA_TQ = 128
DSA_TK = 256


def _dsa_attn_body(qlat_ref, qidx_ref, widxt_ref, ckv_ref, kidx_ref, tab_ref, wuv_ref, o_ref,
                   keys_ref, keyst_ref, acc_ref, m_ref, pstar_ref,
                   s0_ref, s1_ref, p0_ref, p1_ref, a0_ref, a1_ref,
                   *, seq, topk, n_tab):
    tq, tk = DSA_TQ, DSA_TK
    i = pl.program_id(1)
    q0 = i * tq
    nk = (q0 + tq + tk - 1) // tk
    qpos = q0 + lax.broadcasted_iota(I32, (tq, tk), 0)
    col = lax.broadcasted_iota(I32, (tq, tk), 1)
    qpos_t = q0 + lax.broadcasted_iota(I32, (tk, tq), 1)
    krow_t = lax.broadcasted_iota(I32, (tk, tq), 0)

    def dup(v):
        return jnp.concatenate([v] * (tk // LANES), axis=1)

    def score_tile(j, carry):
        k0 = pl.multiple_of(j * tk, tk)
        kt = kidx_ref[pl.ds(k0, tk), :]
        rel = lax.dot_general(kt, qidx_ref[...].reshape(IDX_HEADS * tq, IDX_DIM),
                              (((1,), (1,)), ((), ())), preferred_element_type=F32)
        s = jnp.zeros((tk, tq), F32)
        for h in range(IDX_HEADS):
            s = s + jnp.maximum(rel[:, h * tq:(h + 1) * tq], 0.0) * widxt_ref[h:h + 1, :]
        bits = pltpu.bitcast(s, I32)
        key = bits ^ ((bits >> 31) & 0x7FFFFFFF)
        key = jnp.where(k0 + krow_t <= qpos_t, key, INT_MIN)
        keyst_ref[j] = key
        keys_ref[j] = key.T
        return carry

    def score_pair(jj, carry):
        score_tile(2 * jj, carry)
        return score_tile(2 * jj + 1, carry)

    lax.fori_loop(0, (nk + 1) // 2, score_pair, 0)

    def count(pred):
        def body(jj, acc):
            for j in (2 * jj, 2 * jj + 1):
                ind = jnp.where(pred(keyst_ref[j], j), 1.0, 0.0)
                parts = [ind[r * 8:(r + 1) * 8, :] for r in range(tk // 8)]
                while len(parts) > 1:
                    parts = [a + b for a, b in zip(parts[::2], parts[1::2])]
                acc = acc + parts[0]
            return acc
        acc = lax.fori_loop(0, (nk + 1) // 2, body, jnp.zeros((8, tq), F32))
        return jnp.sum(acc, axis=0, keepdims=True)

    def bit_step(t, carry):
        v, cge = carry
        cand_u = v | jnp.left_shift(jnp.int32(1), 31 - t)
        cand = cand_u ^ INT_MIN
        cnt = count(lambda key, j: key >= cand)
        take = cnt >= topk
        return jnp.where(take, cand_u, v), jnp.where(take, cnt, cge)

    v, cge = lax.fori_loop(0, 32, bit_step,
                           (jnp.zeros((1, tq), I32), jnp.zeros((1, tq), F32)))
    thr_t = v ^ INT_MIN

    pstar_ref[...] = jnp.full((1, tq), seq, I32)
    surplus = jnp.where((v != 0) & (cge > topk), 1.0, 0.0)

    @pl.when(jnp.max(surplus) > 0.0)
    def _():
        cgt = count(lambda key, j: key > thr_t)
        need = topk - cgt

        def pos_step(t, p):
            cand = p | jnp.left_shift(jnp.int32(1), (seq.bit_length() - 1) - t)
            before = count(lambda key, j: (key == thr_t) & (j * tk + krow_t < cand))
            return jnp.where(before < need, cand, p)

        p = lax.fori_loop(0, seq.bit_length(), pos_step, jnp.zeros((1, tq), I32))
        pstar_ref[...] = jnp.where(surplus > 0.0, p, seq)

    def per_row(v_t):
        return dup(jnp.broadcast_to(v_t, (LANES, tq)).T)

    thr_w = per_row(thr_t)
    pstar_w = per_row(pstar_ref[...])

    acc_ref[...] = jnp.zeros_like(acc_ref)
    m_ref[...] = jnp.full_like(m_ref, NEG)
    dq = q0 // LANES

    n_tiles = seq // tk
    qlat2d = lambda: qlat_ref[...].reshape(N_HEADS * tq, KV_LORA)

    def kv_tile(j):
        return ckv_ref[pl.ds(pl.multiple_of(jnp.clip(j, 0, n_tiles - 1) * tk, tk), tk), :]

    def logits(j):
        return lax.dot_general(qlat2d(), kv_tile(j), (((1,), (1,)), ((), ())),
                               preferred_element_type=F32)

    s_bufs, p_bufs, alpha_bufs = (s0_ref, s1_ref), (p0_ref, p1_ref), (a0_ref, a1_ref)
    s0_ref[...] = logits(0)
    p1_ref[...] = jnp.zeros(p1_ref.shape, BF16)
    a1_ref[...] = jnp.ones(a1_ref.shape, F32)

    def attn_step(j, cur):
        prv = 1 - cur
        s_ref, p_ref, alpha_ref = s_bufs[cur], p_bufs[cur], alpha_bufs[cur]
        s_bufs[prv][...] = logits(j + 1)

        kpos = j * tk + col
        key = keys_ref[jnp.minimum(j, nk - 1)]
        sel = (kpos <= qpos) & ((key > thr_w) | ((key == thr_w) & (kpos <= pstar_w)))
        maskadd = jnp.where(sel, 0.0, NEG)
        dbase = dq - j * (tk // LANES) + 1
        didx = [jnp.clip(dbase - c, 0, n_tab - 1) for c in range(tk // LANES)]
        for h in range(N_HEADS):
            rows = slice(h * tq, (h + 1) * tq)
            bias = jnp.concatenate([tab_ref[d, h] for d in didx], axis=1)
            s = s_ref[rows, :] + bias + maskadd
            s_ref[rows, :] = s
            m_old = m_ref[rows, :]
            m_new = jnp.maximum(m_old, jnp.max(s, axis=1, keepdims=True))
            alpha_ref[rows, :] = jnp.exp(m_old - m_new)
            m_ref[rows, :] = m_new

        kv_ext = jnp.concatenate([kv_tile(j - 1), jnp.ones((tk, LANES), BF16)], axis=1)
        pv = jnp.dot(p_bufs[prv][...], kv_ext, preferred_element_type=F32)
        acc_ref[...] = alpha_bufs[prv][...] * acc_ref[...] + pv

        for h in range(N_HEADS):
            rows = slice(h * tq, (h + 1) * tq)
            p_ref[rows, :] = jnp.exp(s_ref[rows, :] - m_ref[rows, :]).astype(BF16)

    def attn_pair(jj, carry):
        attn_step(2 * jj, 0)
        attn_step(2 * jj + 1, 1)
        return carry

    n_pairs = (nk + 1) // 2
    lax.fori_loop(0, n_pairs, attn_pair, 0)
    kv_ext = jnp.concatenate([kv_tile(2 * n_pairs - 1), jnp.ones((tk, LANES), BF16)], axis=1)
    acc_ref[...] = a1_ref[...] * acc_ref[...] + jnp.dot(p1_ref[...], kv_ext,
                                                        preferred_element_type=F32)

    outs = []
    for h in range(N_HEADS):
        rows = slice(h * tq, (h + 1) * tq)
        inv_l = 1.0 / acc_ref[rows, KV_LORA:]
        o_lat = acc_ref[rows, :KV_LORA] * jnp.concatenate([inv_l] * (KV_LORA // LANES), axis=1)
        outs.append(jnp.dot(o_lat.astype(BF16), wuv_ref[h], preferred_element_type=F32))
    o_ref[...] = jnp.concatenate(outs, axis=1).astype(o_ref.dtype)


def dsa_attention(qlat, qidx, widx, ckv, kidx, tab, w_uv, bsz, seq):
    tq, tk = DSA_TQ, DSA_TK
    nq = seq // tq
    n = bsz * seq
    n_tab = tab.shape[0]
    topk = min(IDX_TOPK_MAX, seq // 4)
    body = functools.partial(_dsa_attn_body, seq=seq, topk=topk, n_tab=n_tab)
    return pl.pallas_call(
        body,
        grid=(bsz, nq),
        in_specs=[pl.BlockSpec((N_HEADS, tq, KV_LORA), lambda b, i: (0, b * nq + i, 0)),
                  pl.BlockSpec((IDX_HEADS, tq, IDX_DIM), lambda b, i: (0, b * nq + i, 0)),
                  pl.BlockSpec((IDX_HEADS, tq), lambda b, i: (0, b * nq + i)),
                  pl.BlockSpec((seq, KV_LORA), lambda b, i: (b, 0)),
                  pl.BlockSpec((seq, IDX_DIM), lambda b, i: (b, 0)),
                  _resident(tab.shape, lambda b, i: (0, 0, 0, 0)),
                  _resident((N_HEADS, KV_LORA, HEAD_DIM), lambda b, i: (0, 0, 0))],
        out_specs=pl.BlockSpec((tq, N_HEADS * HEAD_DIM), lambda b, i: (b * nq + i, 0)),
        out_shape=jax.ShapeDtypeStruct((n, N_HEADS * HEAD_DIM), BF16),
        scratch_shapes=[pltpu.VMEM((seq // tk, tq, tk), I32),
                        pltpu.VMEM((seq // tk, tk, tq), I32),
                        pltpu.VMEM((N_HEADS * tq, KV_LORA + LANES), F32),
                        pltpu.VMEM((N_HEADS * tq, 1), F32),
                        pltpu.VMEM((1, tq), I32),
                        pltpu.VMEM((N_HEADS * tq, tk), F32),
                        pltpu.VMEM((N_HEADS * tq, tk), F32),
                        pltpu.VMEM((N_HEADS * tq, tk), BF16),
                        pltpu.VMEM((N_HEADS * tq, tk), BF16),
                        pltpu.VMEM((N_HEADS * tq, 1), F32),
                        pltpu.VMEM((N_HEADS * tq, 1), F32)],
        compiler_params=_cparams("arbitrary", "arbitrary"),
        name="dsa_attention",
    )(qlat, qidx, widx, ckv, kidx, tab, w_uv)


def dsa_bias_buckets(seq):
    far = np.maximum(np.arange(seq), N_BUCKETS // 2).astype(np.float64)
    far_bucket = np.floor(np.log(far / (N_BUCKETS // 2)) / math.log(MAX_DISTANCE / (N_BUCKETS // 2))
                          * (N_BUCKETS - N_BUCKETS // 2))
    unsat = np.nonzero(far_bucket < N_BUCKETS - 1 - N_BUCKETS // 2)[0]
    n_sat = int(unsat[-1]) + 1 if unsat.size else 0
    n_tab = min(seq // LANES + 1, -(-(n_sat + LANES - 1) // LANES) + 2)
    t = np.arange(n_tab)[:, None, None]
    qi = np.arange(LANES)[None, :, None]
    ki = np.arange(LANES)[None, None, :]
    dist = np.maximum(LANES * (t - 1) + qi - ki, 0)
    return _t5_bucket(jnp.asarray(dist, I32))


def _dil_body(q_ref, kp_ref, km_ref, vp_ref, vm_ref, bias_ref, o_ref, lse_ref, *, look):
    nblk = pl.program_id(2)
    tq = q_ref.shape[2]
    ncol = look + tq
    colv = lax.broadcasted_iota(I32, (1, ncol), 1)
    first = jnp.where((nblk == 0) & (colv < look), NEG, 0.0)
    lse_ref[0, 0] = jnp.zeros(lse_ref.shape[2:], F32)
    ones = jnp.ones((ncol, HEAD_DIM), BF16)
    outs = []
    for h in range(N_HEADS):
        sl = slice(h * HEAD_DIM, (h + 1) * HEAD_DIM)
        q = q_ref[0, 0, :, sl]
        k = jnp.concatenate([kp_ref[0, 0, :, sl], km_ref[0, 0, :, sl]], axis=0)
        v = jnp.concatenate([vp_ref[0, 0, :, sl], vm_ref[0, 0, :, sl]], axis=0)
        s = lax.dot_general(q, k, (((1,), (1,)), ((), ())), preferred_element_type=F32)
        s = s + bias_ref[0, h] + first
        m = jnp.max(s, axis=1, keepdims=True)
        p = jnp.exp(s - m)
        ov = jnp.dot(p.astype(BF16), jnp.concatenate([v, ones], axis=1),
                     preferred_element_type=F32)
        l = ov[:, HEAD_DIM:]
        outs.append(ov[:, :HEAD_DIM] * (1.0 / l))
        lse_ref[0, 0, :, h:h + 1] = m + jnp.log(l[:, :1])
    o_ref[0, 0] = jnp.concatenate(outs, axis=1).astype(o_ref.dtype)


def dilated_branch(qkv, bias, look):
    bsz, dil, ls, _ = qkv.shape
    hw = N_HEADS * HEAD_DIM
    tq = bias.shape[2]
    nb = ls // tq
    r = tq // look
    prev = lambda part: (lambda b, c, n: (b, c, jnp.maximum(n * r - 1, 0), part))
    main = lambda part: (lambda b, c, n: (b, c, n, part))
    return pl.pallas_call(
        functools.partial(_dil_body, look=look),
        grid=(bsz, dil, nb),
        in_specs=[pl.BlockSpec((1, 1, tq, hw), main(0)),
                  pl.BlockSpec((1, 1, look, hw), prev(1)),
                  pl.BlockSpec((1, 1, tq, hw), main(1)),
                  pl.BlockSpec((1, 1, look, hw), prev(2)),
                  pl.BlockSpec((1, 1, tq, hw), main(2)),
                  _resident(bias.shape, lambda b, c, n: (0, 0, 0, 0))],
        out_specs=[pl.BlockSpec((1, 1, tq, hw), lambda b, c, n: (b, c, n, 0)),
                   pl.BlockSpec((1, 1, tq, LANES), lambda b, c, n: (b, c, n, 0))],
        out_shape=[jax.ShapeDtypeStruct((bsz, dil, ls, hw), BF16),
                   jax.ShapeDtypeStruct((bsz, dil, ls, LANES), F32)],
        compiler_params=_cparams("arbitrary", "arbitrary", "arbitrary"),
        name=f"dilated_d{dil}",
    )(qkv, qkv, qkv, qkv, qkv, bias)


def dilated_bias_buckets(tq, look, dil):
    rel = np.arange(tq)[:, None] + look - np.arange(look + tq)[None, :]
    band = (rel >= 0) & (rel <= look)
    bucket = _t5_bucket(jnp.asarray(dil * np.maximum(rel, 0), I32))
    return jnp.where(jnp.asarray(band), bucket, MASK_BUCKET)[None]


def _post_attn_body(*refs, dils, alpha):
    n_parts = len(dils)
    merge = n_parts > 1
    o_refs = refs[:n_parts]
    lse_refs = refs[n_parts:2 * n_parts] if merge else ()
    base = n_parts + len(lse_refs)
    wo_ref, x_ref, g_ref, b_ref, xo_ref, xb_ref = refs[base:base + 6]
    scratch = list(refs[base + 6:])
    tm = x_ref.shape[0]
    head = lambda h: slice(h * HEAD_DIM, (h + 1) * HEAD_DIM)
    if not merge:
        o = o_refs[0][0, 0]
    else:
        lses, o_heads = [], []
        for part, dil in enumerate(dils):
            if dil == 1:
                lses.append(lse_refs[part][0, 0])
                o_heads.append(lambda h, r=o_refs[part]: r[0, 0, :, head(h)].astype(F32))
                continue
            o_scr, l_scr = scratch.pop(0), scratch.pop(0)
            rows = tm // dil
            for c in range(dil):
                l_scr[pl.ds(c, rows, stride=dil), :] = lse_refs[part][0, c]
                for h in range(N_HEADS):
                    o_scr[h, pl.ds(c, rows, stride=dil), :] = (
                        o_refs[part][0, c, :, head(h)].astype(F32))
            lses.append(l_scr[...])
            o_heads.append(lambda h, r=o_scr: r[h])
        mx = functools.reduce(jnp.maximum, lses)
        ws = [jnp.exp(v - mx) for v in lses]
        inv = 1.0 / functools.reduce(lambda a, b: a + b, ws)
        ws = [w * inv for w in ws]
        cols = []
        for h in range(N_HEADS):
            acc = None
            for part in range(n_parts):
                term = ws[part][:, h:h + 1] * o_heads[part](h)
                acc = term if acc is None else acc + term
            cols.append(acc)
        o = jnp.concatenate(cols, axis=1).astype(BF16)
    hproj = jnp.dot(o, wo_ref[...], preferred_element_type=F32)
    y = _layer_norm(alpha * x_ref[...] + hproj, g_ref[...], b_ref[...])
    xo_ref[...] = y
    xb_ref[...] = y.astype(BF16)


def post_attention(o_parts, lse_parts, w_o, x, g, b, alpha, tm):
    n, d = x.shape
    hw = N_HEADS * HEAD_DIM
    dils = tuple(o.shape[1] for o in o_parts)
    seq = o_parts[0].shape[1] * o_parts[0].shape[2]
    tpb = seq // tm
    row = lambda width: pl.BlockSpec((tm, width), lambda i: (i, 0))
    by_class = lambda dil, width: pl.BlockSpec((1, dil, tm // dil, width),
                                               lambda i: (i // tpb, 0, i % tpb, 0))
    in_specs = ([by_class(dil, hw) for dil in dils]
                + [by_class(dil, LANES) for dil in dils[:len(lse_parts)]]
                + [_resident((hw, d), lambda i: (0, 0)), row(d),
                   _resident((1, d), lambda i: (0, 0)), _resident((1, d), lambda i: (0, 0))])
    scratch = []
    for dil in dils:
        if dil > 1:
            scratch += [pltpu.VMEM((N_HEADS, tm, HEAD_DIM), F32), pltpu.VMEM((tm, LANES), F32)]
    return pl.pallas_call(
        functools.partial(_post_attn_body, dils=dils, alpha=alpha),
        grid=(n // tm,),
        in_specs=in_specs,
        out_specs=[row(d), row(d)],
        out_shape=[jax.ShapeDtypeStruct((n, d), F32), jax.ShapeDtypeStruct((n, d), BF16)],
        scratch_shapes=scratch,
        compiler_params=_cparams("arbitrary"),
        name=f"post_attention_{len(dils)}",
    )(*o_parts, *lse_parts, w_o, x, g, b)


def _router_body(x_ref, rw_ref, rb_ref, route_ref, gates_ref, counts_ref, carry_ref):
    tm = x_ref.shape[0]

    @pl.when(pl.program_id(0) == 0)
    def _():
        carry_ref[...] = jnp.zeros_like(carry_ref)

    x = x_ref[...]
    x_hi = x.astype(BF16)
    x_lo = (x - x_hi.astype(F32)).astype(BF16)
    rw = rw_ref[...]
    both = jnp.dot(x_hi, rw, preferred_element_type=F32)
    logits = (both[:, :LANES] + both[:, LANES:]
              + jnp.dot(x_lo, rw[:, :LANES], preferred_element_type=F32)).T
    rows = [logits[e:e + 1, :] for e in range(N_EXPERTS)]
    mx = functools.reduce(jnp.maximum, rows)
    ex = [jnp.exp(r - mx) for r in rows]
    inv = 1.0 / functools.reduce(lambda a, b: a + b, ex)
    probs = [e * inv for e in ex]
    sel = [probs[e] + rb_ref[e] for e in range(N_EXPERTS)]

    zero_i = jnp.zeros((1, tm), I32)
    best_t = None
    for g in range(N_GROUPS):
        vals = sel[g * EXPERTS_PER_GROUP:(g + 1) * EXPERTS_PER_GROUP]
        prb = probs[g * EXPERTS_PER_GROUP:(g + 1) * EXPERTS_PER_GROUP]
        top, top_i, top_p = vals[0], zero_i, prb[0]
        for j in range(1, EXPERTS_PER_GROUP):
            c = vals[j] > top
            top = jnp.where(c, vals[j], top)
            top_i = jnp.where(c, j, top_i)
            top_p = jnp.where(c, prb[j], top_p)
        sec = jnp.full((1, tm), -jnp.inf, F32)
        sec_i, sec_p = zero_i, jnp.zeros((1, tm), F32)
        for j in range(EXPERTS_PER_GROUP):
            c = (vals[j] > sec) & (top_i != j)
            sec = jnp.where(c, vals[j], sec)
            sec_i = jnp.where(c, j, sec_i)
            sec_p = jnp.where(c, prb[j], sec_p)
        tot = top + sec
        if best_t is None:
            best_t = tot
            e0, e1 = top_i + g * EXPERTS_PER_GROUP, sec_i + g * EXPERTS_PER_GROUP
            p0, p1 = top_p, sec_p
        else:
            c = tot > best_t
            best_t = jnp.where(c, tot, best_t)
            e0 = jnp.where(c, top_i + g * EXPERTS_PER_GROUP, e0)
            e1 = jnp.where(c, sec_i + g * EXPERTS_PER_GROUP, e1)
            p0 = jnp.where(c, top_p, p0)
            p1 = jnp.where(c, sec_p, p1)
    ginv = 1.0 / (p0 + p1)
    g0, g1 = p0 * ginv, p1 * ginv
    gates_ref[...] = jnp.concatenate([g0, g1, jnp.zeros((LANES - 2, tm), F32)], axis=0).T

    assign = jnp.concatenate([jnp.where((e0 == e) | (e1 == e), 1.0, 0.0)
                              for e in range(N_EXPERTS)], axis=0).astype(BF16)
    upper = jnp.where(lax.broadcasted_iota(I32, (tm, tm), 0)
                      <= lax.broadcasted_iota(I32, (tm, tm), 1), 1.0, 0.0).astype(BF16)
    prefix = jnp.dot(assign, upper, preferred_element_type=F32) + carry_ref[...]
    rank0 = jnp.zeros((1, tm), F32)
    rank1 = jnp.zeros((1, tm), F32)
    for e in range(N_EXPERTS):
        rank0 = jnp.where(e0 == e, prefix[e:e + 1, :], rank0)
        rank1 = jnp.where(e1 == e, prefix[e:e + 1, :], rank1)
    total = prefix[:, tm - 1:tm]
    carry_ref[...] = total
    counts_ref[...] = jnp.broadcast_to(total, counts_ref.shape).astype(I32)
    route_ref[...] = jnp.concatenate(
        [e0, e1, (rank0 - 1.0).astype(I32), (rank1 - 1.0).astype(I32), jnp.zeros((4, tm), I32)],
        axis=0)


def router(x, router_wp, router_b, tm):
    n, d = x.shape
    return pl.pallas_call(
        _router_body,
        grid=(n // tm,),
        in_specs=[pl.BlockSpec((tm, d), lambda i: (i, 0)),
                  _resident((d, 2 * LANES), lambda i: (0, 0)),
                  pl.BlockSpec(memory_space=pltpu.SMEM)],
        out_specs=[pl.BlockSpec((8, tm), lambda i: (0, i)),
                   pl.BlockSpec((tm, LANES), lambda i: (i, 0)),
                   pl.BlockSpec((N_EXPERTS, LANES), lambda i: (0, 0))],
        out_shape=[jax.ShapeDtypeStruct((8, n), I32),
                   jax.ShapeDtypeStruct((n, LANES), F32),
                   jax.ShapeDtypeStruct((N_EXPERTS, LANES), I32)],
        scratch_shapes=[pltpu.VMEM((N_EXPERTS, 1), F32)],
        compiler_params=_cparams("arbitrary"),
        name="router",
    )(x, router_wp, router_b)


EXPERT_TILE = 512
COMBINE_TILE = 256


def _row_copy(src_hbm, row, dst, dst_row, sem):
    return pltpu.make_async_copy(src_hbm.at[pl.ds(row, 1), :], dst.at[pl.ds(dst_row, 1), :], sem)


def _experts_body(te_ref, nu_ref, src_next_ref, src_first_ref, x_hbm, wg_ref, wu_ref, wd_ref,
                  ys_ref, xbuf, sem, wgb, wub, wdb):
    i = pl.program_id(0)
    n_used = nu_ref[0]
    slot = i % 2
    tmx = ys_ref.shape[0]

    def issue(src_ref, dst_slot):
        for r in range(tmx):
            _row_copy(x_hbm, src_ref[0, 0, r], xbuf.at[dst_slot], r, sem.at[dst_slot]).start()

    @pl.when(i == 0)
    def _():
        issue(src_first_ref, 0)

    @pl.when(i + 1 < n_used)
    def _():
        issue(src_next_ref, 1 - slot)

    @pl.when(i < n_used)
    def _():
        def wait_row(r, c):
            _row_copy(x_hbm, 0, xbuf.at[slot], r, sem.at[slot]).wait()
            return c
        lax.fori_loop(0, tmx, wait_row, 0, unroll=8)

        @pl.when((i == 0) | (te_ref[i] != te_ref[jnp.maximum(i - 1, 0)]))
        def _():
            wgb[...] = wg_ref[0, 0].astype(BF16)
            wub[...] = wu_ref[0, 0].astype(BF16)
            wdb[...] = wd_ref[0, 0].astype(BF16)

        xb = xbuf[slot].astype(BF16)
        gate = jnp.dot(xb, wgb[...], preferred_element_type=F32)
        up = jnp.dot(xb, wub[...], preferred_element_type=F32)
        hidden = (gate * jax.nn.sigmoid(gate)) * up
        ys_ref[...] = jnp.dot(hidden.astype(BF16), wdb[...], preferred_element_type=F32)

    @pl.when(i >= n_used)
    def _():
        ys_ref[...] = jnp.zeros_like(ys_ref)


def moe_experts(x, src_tiles, tile_expert, n_used, w_gate, w_up, w_down, layer):
    n, d = x.shape
    n_tiles = src_tiles.shape[0]
    tmx = EXPERT_TILE
    smem_tile = lambda index_map: pl.BlockSpec((1, 1, tmx), index_map, memory_space=pltpu.SMEM)
    by_expert = lambda shape: pl.BlockSpec((1,) + shape, lambda i, te, nu: (layer, te[i], 0, 0))
    grid_spec = pltpu.PrefetchScalarGridSpec(
        num_scalar_prefetch=2,
        grid=(n_tiles,),
        in_specs=[smem_tile(lambda i, te, nu: (jnp.minimum(i + 1, n_tiles - 1), 0, 0)),
                  smem_tile(lambda i, te, nu: (0, 0, 0)),
                  pl.BlockSpec(memory_space=pl.ANY),
                  by_expert((1, d, D_EXPERT)), by_expert((1, d, D_EXPERT)),
                  by_expert((1, D_EXPERT, d))],
        out_specs=pl.BlockSpec((tmx, d), lambda i, te, nu: (i, 0)),
        scratch_shapes=[pltpu.VMEM((2, tmx, d), F32),
                        pltpu.SemaphoreType.DMA((2,)),
                        pltpu.VMEM((d, D_EXPERT), BF16),
                        pltpu.VMEM((d, D_EXPERT), BF16),
                        pltpu.VMEM((D_EXPERT, d), BF16)])
    return pl.pallas_call(
        _experts_body,
        grid_spec=grid_spec,
        out_shape=jax.ShapeDtypeStruct((n_tiles * tmx, d), F32),
        compiler_params=_cparams("arbitrary"),
        name="moe_experts",
    )(tile_expert, n_used, src_tiles, src_tiles, x, w_gate, w_up, w_down)


def _combine_body(pos_next_ref, pos_first_ref, ys_hbm, gates_ref, x_ref, g_ref, b_ref,
                  xo_ref, xbo_ref, ybuf, sem, *, alpha):
    i = pl.program_id(0)
    n_steps = pl.num_programs(0)
    slot = i % 2
    tm = x_ref.shape[0]

    def issue(pos_ref, dst_slot):
        for k in range(2):
            for r in range(tm):
                _row_copy(ys_hbm, pos_ref[0, k, r], ybuf.at[dst_slot, k], r,
                          sem.at[dst_slot]).start()

    @pl.when(i == 0)
    def _():
        issue(pos_first_ref, 0)

    @pl.when(i + 1 < n_steps)
    def _():
        issue(pos_next_ref, 1 - slot)

    def wait_row(r, c):
        for k in range(2):
            _row_copy(ys_hbm, 0, ybuf.at[slot, k], r, sem.at[slot]).wait()
        return c
    lax.fori_loop(0, tm, wait_row, 0, unroll=8)

    gates = gates_ref[...]
    y = gates[:, 0:1] * ybuf[slot, 0] + gates[:, 1:2] * ybuf[slot, 1]
    out = _layer_norm(alpha * x_ref[...] + y, g_ref[...], b_ref[...])
    xo_ref[...] = out
    xbo_ref[...] = out.astype(BF16)


def moe_combine(ys, pos_tiles, gates, x, g, b, alpha):
    n, d = x.shape
    tm = COMBINE_TILE
    n_steps = n // tm
    smem_tile = lambda index_map: pl.BlockSpec((1, 2, tm), index_map, memory_space=pltpu.SMEM)
    row = lambda width: pl.BlockSpec((tm, width), lambda i: (i, 0))
    return pl.pallas_call(
        functools.partial(_combine_body, alpha=alpha),
        grid=(n_steps,),
        in_specs=[smem_tile(lambda i: (jnp.minimum(i + 1, n_steps - 1), 0, 0)),
                  smem_tile(lambda i: (0, 0, 0)),
                  pl.BlockSpec(memory_space=pl.ANY),
                  row(LANES), row(d),
                  _resident((1, d), lambda i: (0, 0)), _resident((1, d), lambda i: (0, 0))],
        out_specs=[row(d), row(d)],
        out_shape=[jax.ShapeDtypeStruct((n, d), F32), jax.ShapeDtypeStruct((n, d), BF16)],
        scratch_shapes=[pltpu.VMEM((2, 2, tm, d), F32), pltpu.SemaphoreType.DMA((2,))],
        compiler_params=_cparams("arbitrary"),
        name="moe_combine",
    )(pos_tiles, pos_tiles, ys, gates, x, g, b)


def moe_layer(x, router_wp, router_b, w_gate, w_up, w_down, layer, g, b, alpha):
    n, d = x.shape
    tmx = EXPERT_TILE
    n_tiles = (2 * n) // tmx + N_EXPERTS
    route, gates, counts = router(x, router_wp, router_b, tm=512)
    counts = counts[:, 0]
    tiles_e = (counts + tmx - 1) // tmx
    tile_end = jnp.cumsum(tiles_e)
    row_off = (tile_end - tiles_e) * tmx
    pos0 = row_off[route[0]] + route[2]
    pos1 = row_off[route[1]] + route[3]
    tok = jnp.arange(n, dtype=I32)
    src = jnp.zeros((n_tiles * tmx,), I32).at[jnp.concatenate([pos0, pos1])].set(
        jnp.concatenate([tok, tok]), unique_indices=True)
    tile_expert = jnp.minimum(
        jnp.sum(jnp.arange(n_tiles, dtype=I32)[:, None] >= tile_end[None, :], axis=1),
        N_EXPERTS - 1).astype(I32)
    n_used = tile_end[-1:].astype(I32)
    ys = moe_experts(x, src.reshape(n_tiles, 1, tmx), tile_expert, n_used, w_gate, w_up, w_down,
                     layer)
    tm = COMBINE_TILE
    pos_tiles = jnp.stack([pos0, pos1]).reshape(2, n // tm, tm).transpose(1, 0, 2)
    return moe_combine(ys, pos_tiles, gates, x, g, b, alpha)


def kernel(x, rel_bias, router_w, router_b, a_w_in, a_q_norm, a_kv_norm, a_w_uq, a_w_uq_idx,
           a_w_uk, a_w_uv, a_w_o, b_w_in, b_w_o, ln1_g, ln1_b, ln2_g, ln2_b,
           moe_w_gate, moe_w_up, moe_w_down):
    bsz, seq, d = x.shape
    depth = ln1_g.shape[0]
    n = bsz * seq
    hw = N_HEADS * HEAD_DIM
    alpha = (2 * depth) ** 0.25

    x = x.reshape(n, d)
    xb = x.astype(BF16)
    rw = jnp.pad(router_w, ((0, 0), (0, LANES - N_EXPERTS)))
    rw_hi = rw.astype(BF16)
    router_wp = jnp.concatenate([rw_hi, (rw - rw_hi.astype(F32)).astype(BF16)], axis=1)

    dsa_tab = bias_expand(dsa_bias_buckets(seq), rel_bias)
    dil_bias = []
    for window, dil in DILATED_PATTERNS:
        look = window // dil
        tq = min(2 * look, seq // dil)
        dil_bias.append(bias_expand(dilated_bias_buckets(tq, look, dil), rel_bias))

    for i in range(depth):
        j = i // 2
        if i % 2 == 0:
            w_in = a_w_in[j]
            kv_end = Q_LORA + KV_LORA
            w_in_p = jnp.concatenate(
                [w_in[:, :kv_end + IDX_DIM], jnp.zeros((d, LANES - IDX_DIM), F32),
                 w_in[:, kv_end + IDX_DIM:], jnp.zeros((d, LANES - IDX_HEADS), F32)],
                axis=1).astype(BF16)
            qlat, qidx, widx, ckv, kidx = dsa_prologue(
                xb, w_in_p, a_q_norm[j][None], a_kv_norm[j][None], a_w_uq[j].astype(BF16),
                a_w_uq_idx[j].astype(BF16), a_w_uk[j].astype(BF16), tm=256)
            o = dsa_attention(qlat, qidx, widx, ckv, kidx, dsa_tab, a_w_uv[j].astype(BF16),
                              bsz, seq)
            o_parts, lse_parts, w_o = [o.reshape(bsz, 1, seq, hw)], [], a_w_o[j]
        else:
            dils = tuple(dil for _, dil in DILATED_PATTERNS)
            qkvs = qkv_projection(xb, b_w_in[j].astype(BF16), bsz, seq, dils, tm=512, tn=1024,
                                  scaled_blocks=hw // 1024, scale=HEAD_DIM ** -0.5)
            o_parts, lse_parts = [], []
            for (window, dil), qkv, bias in zip(DILATED_PATTERNS, qkvs, dil_bias):
                o_i, lse_i = dilated_branch(qkv, bias, window // dil)
                o_parts.append(o_i)
                lse_parts.append(lse_i)
            w_o = b_w_o[j]
        x, xb = post_attention(o_parts, lse_parts, w_o.astype(BF16), x, ln1_g[i][None],
                               ln1_b[i][None], alpha, tm=256)
        x, xb = moe_layer(x, router_wp, router_b, moe_w_gate, moe_w_up, moe_w_down, i,
                          ln2_g[i][None], ln2_b[i][None], alpha)
    return x.reshape(bsz, seq, d)
```
